```python
import math, functools
import jax
import jax.numpy as jnp
from jax import lax
import numpy as np

D_MODEL = 1024
BATCH = 4
SEQ = 4096
DEPTH = 4
DEC_BATCH = 32
DEC_SEQ = 4
PAST_LEN = 8192
PAGE_SIZE = 128

MIX_W = D_MODEL
ATT_W = MIX_W // 2
SSD_W = MIX_W // 4
POOL_W = MIX_W - ATT_W - SSD_W
HEAD_DIM = 64
N_HEADS = ATT_W // HEAD_DIM
MOBA_BLOCK = 256
MOBA_TOPK = 3
Q_CHUNK = 64
ATT_SCALE = HEAD_DIM ** -0.5
SSD_HEADDIM = 64
SSD_HEADS = SSD_W // SSD_HEADDIM
SSD_GROUPS = 2
HEADS_PER_GROUP = SSD_HEADS // SSD_GROUPS
D_STATE = 128
SSD_BC = SSD_GROUPS * D_STATE
CONV_W = 4
CONV_DIM = SSD_W + 2 * SSD_BC
SSD_CHUNK = 128
POOL_WINDOWS = (2, 4, 8, 16)
POOL_GROUPS = len(POOL_WINDOWS)
POOL_GROUP_W = POOL_W // POOL_GROUPS
POOL_HIST = max(POOL_WINDOWS) - 1
D_FF = -(-8 * D_MODEL // (3 * 256)) * 256
IN_W = 3 * ATT_W + SSD_W + CONV_DIM + SSD_HEADS + POOL_W
IN_SPLITS = [ATT_W, 2 * ATT_W, 3 * ATT_W, 3 * ATT_W + SSD_W, 3 * ATT_W + SSD_W + CONV_DIM,
             3 * ATT_W + SSD_W + CONV_DIM + SSD_HEADS]
RMS_EPS = 1e-6

kernel_name = 'hymba_moba_ssd_pool_decoder_step'


def rmsnorm(x, g):
    x32 = x.astype(jnp.float32)
    y = x32 * lax.rsqrt(jnp.mean(x32 * x32, axis=-1, keepdims=True) + RMS_EPS)
    return (y * g.astype(jnp.float32)).astype(x.dtype)


def alibi_slopes():
    return jnp.exp2(-8.0 * jnp.arange(1, N_HEADS + 1, dtype=jnp.float32) / N_HEADS)


def to_blocks(k):
    n, length = k.shape[0], k.shape[1]
    nb = max(-(-length // MOBA_BLOCK), MOBA_TOPK)
    k = jnp.pad(k, ((0, 0), (0, nb * MOBA_BLOCK - length), (0, 0), (0, 0)))
    return k.reshape(n, nb, MOBA_BLOCK, N_HEADS, HEAD_DIM).transpose(0, 3, 1, 2, 4)


def moba_attend(q, pos_q, kb, vb, means):
    n, t = q.shape[0], q.shape[1]
    nb = kb.shape[2]
    own = pos_q // MOBA_BLOCK
    gate = jnp.einsum('nthd,nhbd->nhtb', q, means, preferred_element_type=jnp.float32)
    past = jnp.arange(nb)[None, :] < own[:, None]
    gate = jnp.where(past, gate, -jnp.inf)
    _, sel = lax.top_k(gate, MOBA_TOPK)
    valid = jnp.arange(MOBA_TOPK)[None, :] < own[:, None]
    n_idx = jnp.arange(n)[:, None, None, None]
    h_idx = jnp.arange(N_HEADS)[None, :, None, None]
    k_sel = kb[n_idx, h_idx, sel]
    v_sel = vb[n_idx, h_idx, sel]
    k_own = kb[:, :, own]
    v_own = vb[:, :, own]
    slopes = alibi_slopes()
    offs = jnp.arange(MOBA_BLOCK)
    sel_pos = sel[..., None] * MOBA_BLOCK + offs
    own_pos = own[:, None] * MOBA_BLOCK + offs
    s_sel = (jnp.einsum('nthd,nhtkbd->nhtkb', q, k_sel, preferred_element_type=jnp.float32) * ATT_SCALE
             - slopes[:, None, None, None] * (pos_q[:, None, None] - sel_pos).astype(jnp.float32))
    s_sel = jnp.where(valid[:, :, None], s_sel, -jnp.inf)
    s_own = (jnp.einsum('nthd,nhtbd->nhtb', q, k_own, preferred_element_type=jnp.float32) * ATT_SCALE
             - slopes[:, None, None] * (pos_q[:, None] - own_pos).astype(jnp.float32))
    s_own = jnp.where(own_pos <= pos_q[:, None], s_own, -jnp.inf)
    n_sel = MOBA_TOPK * MOBA_BLOCK
    logits = jnp.concatenate([s_sel.reshape(n, N_HEADS, t, n_sel), s_own], axis=-1)
    p = jax.nn.softmax(logits, axis=-1)
    p_sel = p[..., :n_sel].reshape(n, N_HEADS, t, MOBA_TOPK, MOBA_BLOCK)
    p_own = p[..., n_sel:]
    out = (jnp.einsum('nhtkb,nhtkbd->nthd', p_sel, v_sel, preferred_element_type=jnp.float32)
           + jnp.einsum('nhtb,nhtbd->nthd', p_own, v_own, preferred_element_type=jnp.float32))
    return out.astype(q.dtype)


def prompt_attend(q, k, v):
    n, s = q.shape[0], q.shape[1]
    kb, vb = to_blocks(k), to_blocks(v)
    means = jnp.mean(kb, axis=3, dtype=jnp.float32)
    nch = s // Q_CHUNK
    qc = q.reshape(n, nch, Q_CHUNK, N_HEADS, HEAD_DIM).transpose(1, 0, 2, 3, 4)
    pc = jnp.arange(s, dtype=jnp.int32).reshape(nch, Q_CHUNK)
    out = lax.map(lambda qp: moba_attend(qp[0], qp[1], kb, vb, means), (qc, pc))
    return out.transpose(1, 0, 2, 3, 4).reshape(n, s, N_HEADS, HEAD_DIM)


def sample_attend(q, k, v, cache_k_l, cache_v_l, page_table):
    n, t = q.shape[0], q.shape[1]
    past_k = cache_k_l[page_table].reshape(n, -1, N_HEADS, HEAD_DIM)
    past_v = cache_v_l[page_table].reshape(n, -1, N_HEADS, HEAD_DIM)
    past = past_k.shape[1]
    kb = to_blocks(jnp.concatenate([past_k, k.astype(past_k.dtype)], axis=1))
    vb = to_blocks(jnp.concatenate([past_v, v.astype(past_v.dtype)], axis=1))
    means = jnp.mean(kb, axis=3, dtype=jnp.float32)
    pos = past + jnp.arange(t, dtype=jnp.int32)
    return moba_attend(q, pos, kb, vb, means)


def ssd_scan(x, dt, a, bm, cm, h0):
    n, t = x.shape[0], x.shape[1]
    chunk = SSD_CHUNK if t % SSD_CHUNK == 0 else t
    nc = t // chunk
    bm = jnp.repeat(bm.astype(jnp.float32), HEADS_PER_GROUP, axis=2).reshape(n, nc, chunk, SSD_HEADS, D_STATE)
    cm = jnp.repeat(cm.astype(jnp.float32), HEADS_PER_GROUP, axis=2).reshape(n, nc, chunk, SSD_HEADS, D_STATE)
    xd = (x * dt[..., None]).reshape(n, nc, chunk, SSD_HEADS, SSD_HEADDIM)
    a_cum = lax.cumsum((dt * a).reshape(n, nc, chunk, SSD_HEADS), axis=2)
    causal = jnp.tril(jnp.ones((chunk, chunk), dtype=bool))
    diff = a_cum[:, :, :, None, :] - a_cum[:, :, None, :, :]
    lmat = jnp.exp(jnp.where(causal[:, :, None], diff, -jnp.inf))
    g = jnp.einsum('nclhk,ncshk->nclsh', cm, bm) * lmat
    y_diag = jnp.einsum('nclsh,ncshp->nclhp', g, xd)
    decay = jnp.exp(a_cum[:, :, -1:, :] - a_cum)
    states = jnp.einsum('nclhk,nclhp->nchpk', bm * decay[..., None], xd)
    chunk_decay = jnp.exp(a_cum[:, :, -1, :])

    def step(h, inp):
        dec, st = inp
        return dec[:, :, None, None] * h + st, h

    h_final, prev = lax.scan(step, h0.astype(jnp.float32),
                             (jnp.moveaxis(chunk_decay, 1, 0), jnp.moveaxis(states, 1, 0)))
    prev = jnp.moveaxis(prev, 0, 1)
    y_off = jnp.einsum('nclhk,nchpk->nclhp', cm * jnp.exp(a_cum)[..., None], prev)
    return (y_diag + y_off).reshape(n, t, SSD_HEADS, SSD_HEADDIM), h_final


def pool_mix(u_ext, pos, pool_w, pool_scale):
    n, length, c = u_ext.shape
    t = length - POOL_HIST
    cs = lax.cumsum(u_ext.astype(jnp.float32), axis=1)
    cs = jnp.concatenate([jnp.zeros((n, 1, c), jnp.float32), cs], axis=1)
    u = u_ext[:, POOL_HIST:].astype(jnp.float32)
    hi = cs[:, POOL_HIST + 1:]
    diffs = []
    for gi, w in enumerate(POOL_WINDOWS):
        sl = slice(gi * POOL_GROUP_W, (gi + 1) * POOL_GROUP_W)
        lo = cs[:, POOL_HIST + 1 - w:POOL_HIST + 1 - w + t, sl]
        cnt = jnp.minimum(w, pos + 1).astype(jnp.float32)[:, None]
        diffs.append((hi[..., sl] - lo) / cnt - u[..., sl])
    d = jnp.stack(diffs, axis=2)
    y = jnp.einsum('ntgc,gce->ntge', d, pool_w.astype(jnp.float32)).reshape(n, t, POOL_W)
    return (y * pool_scale.astype(jnp.float32)).astype(u_ext.dtype)


def layer(x, pos, attend, conv_prev, ssm_prev, pool_prev, p):
    (g1, w_in, q_g, k_g, c_w, c_b, dt_b, a_lg, d_sk, ssd_g, p_w, p_s, w_o, g2, w_gu, w_dn) = p
    n, t = x.shape[0], x.shape[1]
    xn = rmsnorm(x, g1)
    q, k, v, z, xbc, dt, pu = jnp.split(xn @ w_in, IN_SPLITS, axis=-1)
    q = rmsnorm(q.reshape(n, t, N_HEADS, HEAD_DIM), q_g)
    k = rmsnorm(k.reshape(n, t, N_HEADS, HEAD_DIM), k_g)
    v = v.reshape(n, t, N_HEADS, HEAD_DIM)
    att = attend(q, k, v).reshape(n, t, ATT_W)
    xbc_ext = jnp.concatenate([conv_prev.astype(xbc.dtype), xbc], axis=1)
    conv = c_b + xbc_ext[:, 0:t] * c_w[0]
    for i in range(1, CONV_W):
        conv = conv + xbc_ext[:, i:i + t] * c_w[i]
    conv = jax.nn.silu(conv)
    xs, bm, cm = jnp.split(conv, [SSD_W, SSD_W + SSD_BC], axis=-1)
    xs = xs.reshape(n, t, SSD_HEADS, SSD_HEADDIM).astype(jnp.float32)
    dt = jax.nn.softplus(dt.astype(jnp.float32) + dt_b.astype(jnp.float32))
    a = -jnp.exp(a_lg.astype(jnp.float32))
    y, ssm_new = ssd_scan(xs, dt, a, bm.reshape(n, t, SSD_GROUPS, D_STATE),
                          cm.reshape(n, t, SSD_GROUPS, D_STATE), ssm_prev)
    y = y + d_sk.astype(jnp.float32)[:, None] * xs
    y = rmsnorm(y.reshape(n, t, SSD_W) * jax.nn.silu(z.astype(jnp.float32)), ssd_g).astype(x.dtype)
    pu_ext = jnp.concatenate([pool_prev.astype(pu.dtype), pu], axis=1)
    pool = pool_mix(pu_ext, pos, p_w, p_s)
    h = x + jnp.concatenate([att, y, pool], axis=-1) @ w_o
    gate, up = jnp.split(rmsnorm(h, g2) @ w_gu, 2, axis=-1)
    out = h + (jax.nn.silu(gate) * up) @ w_dn
    return out, k, v, xbc_ext[:, -(CONV_W - 1):], ssm_new, pu_ext[:, -POOL_HIST:]


def setup_inputs(seed: int = 0) -> dict:
    key = jax.random.key(seed)
    ks = jax.random.split(key, 26)
    f32 = jnp.float32
    n_pages = PAST_LEN // PAGE_SIZE
    n_used = DEC_BATCH * n_pages
    n_pool = n_used + max(1, n_used // 4)

    def nrm(k, shape, scale):
        return scale * jax.random.normal(k, shape, f32)

    x_prompt = nrm(ks[0], (BATCH, SEQ, D_MODEL), 1.0)
    x_sample = nrm(ks[1], (DEC_BATCH, DEC_SEQ, D_MODEL), 1.0)
    cache_k = nrm(ks[2], (DEPTH, n_pool, PAGE_SIZE, N_HEADS, HEAD_DIM), 1.0)
    cache_v = nrm(ks[3], (DEPTH, n_pool, PAGE_SIZE, N_HEADS, HEAD_DIM), 1.0)
    page_table = jax.random.permutation(ks[4], n_pool)[:n_used].reshape(DEC_BATCH, n_pages).astype(jnp.int32)
    state_ssm = nrm(ks[5], (DEPTH, DEC_BATCH, SSD_HEADS, SSD_HEADDIM, D_STATE), 0.1)
    state_conv = nrm(ks[6], (DEPTH, DEC_BATCH, CONV_W - 1, CONV_DIM), 1.0)
    state_pool = nrm(ks[7], (DEPTH, DEC_BATCH, POOL_HIST, POOL_W), 1.0)
    norm1 = 1.0 + nrm(ks[8], (DEPTH, D_MODEL), 0.02)
    w_in = nrm(ks[9], (DEPTH, D_MODEL, IN_W), D_MODEL ** -0.5)
    q_norm = 1.0 + nrm(ks[10], (DEPTH, HEAD_DIM), 0.02)
    k_norm = 1.0 + nrm(ks[11], (DEPTH, HEAD_DIM), 0.02)
    conv_w = nrm(ks[12], (DEPTH, CONV_W, CONV_DIM), CONV_W ** -0.5)
    conv_b = nrm(ks[13], (DEPTH, CONV_DIM), 0.02)
    dt0 = jnp.exp(jax.random.uniform(ks[14], (DEPTH, SSD_HEADS), f32, math.log(1e-3), math.log(1e-1)))
    dt_bias = dt0 + jnp.log(-jnp.expm1(-dt0))
    a_log = jnp.log(jax.random.uniform(ks[15], (DEPTH, SSD_HEADS), f32, 1.0, 16.0))
    d_skip = 1.0 + nrm(ks[16], (DEPTH, SSD_HEADS), 0.1)
    ssd_norm = 1.0 + nrm(ks[17], (DEPTH, SSD_W), 0.02)
    pool_w = nrm(ks[18], (DEPTH, POOL_GROUPS, POOL_GROUP_W, POOL_GROUP_W), POOL_GROUP_W ** -0.5)
    pool_scale = 1.0 + nrm(ks[19], (DEPTH, POOL_W), 0.02)
    w_out = nrm(ks[20], (DEPTH, MIX_W, D_MODEL), 0.5 * MIX_W ** -0.5)
    norm2 = 1.0 + nrm(ks[21], (DEPTH, D_MODEL), 0.02)
    w_gu = nrm(ks[22], (DEPTH, D_MODEL, 2 * D_FF), D_MODEL ** -0.5)
    w_down = nrm(ks[23], (DEPTH, D_FF, D_MODEL), 0.5 * D_FF ** -0.5)
    return {'x_prompt': x_prompt, 'x_sample': x_sample, 'cache_k': cache_k, 'cache_v': cache_v,
            'page_table': page_table, 'state_ssm': state_ssm, 'state_conv': state_conv,
            'state_pool': state_pool, 'norm1': norm1, 'w_in': w_in, 'q_norm': q_norm, 'k_norm': k_norm,
            'conv_w': conv_w, 'conv_b': conv_b, 'dt_bias': dt_bias, 'a_log': a_log, 'd_skip': d_skip,
            'ssd_norm': ssd_norm, 'pool_w': pool_w, 'pool_scale': pool_scale, 'w_out': w_out,
            'norm2': norm2, 'w_gu': w_gu, 'w_down': w_down}


def reference(x_prompt, x_sample, cache_k, cache_v, page_table, state_ssm, state_conv, state_pool,
              norm1, w_in, q_norm, k_norm, conv_w, conv_b, dt_bias, a_log, d_skip, ssd_norm,
              pool_w, pool_scale, w_out, norm2, w_gu, w_down):
    b, s = x_prompt.shape[0], x_prompt.shape[1]
    pos_p = jnp.arange(s, dtype=jnp.int32)
    pos_s = page_table.shape[1] * PAGE_SIZE + jnp.arange(x_sample.shape[1], dtype=jnp.int32)
    conv0 = jnp.zeros((b, CONV_W - 1, CONV_DIM), x_prompt.dtype)
    ssm0 = jnp.zeros((b, SSD_HEADS, SSD_HEADDIM, D_STATE), jnp.float32)
    pool0 = jnp.zeros((b, POOL_HIST, POOL_W), x_prompt.dtype)
    yp, ys = x_prompt, x_sample
    kp_l, vp_l, ks_l, vs_l = [], [], [], []
    hp_l, hs_l, cp_l, cs_l, pp_l, ps_l = [], [], [], [], [], []
    for l in range(DEPTH):
        p = (norm1[l], w_in[l], q_norm[l], k_norm[l], conv_w[l], conv_b[l], dt_bias[l], a_log[l],
             d_skip[l], ssd_norm[l], pool_w[l], pool_scale[l], w_out[l], norm2[l], w_gu[l], w_down[l])
        yp, kp, vp, cp, hp, pp = layer(yp, pos_p, prompt_attend, conv0, ssm0, pool0, p)
        attend_s = functools.partial(sample_attend, cache_k_l=cache_k[l], cache_v_l=cache_v[l],
                                     page_table=page_table)
        ys, kk, vv, cc, hh, pq = layer(ys, pos_s, attend_s, state_conv[l], state_ssm[l], state_pool[l], p)
        kp_l.append(kp); vp_l.append(vp); ks_l.append(kk); vs_l.append(vv)
        hp_l.append(hp); hs_l.append(hh); cp_l.append(cp); cs_l.append(cc); pp_l.append(pp); ps_l.append(pq)
    return (yp, ys, jnp.stack(kp_l), jnp.stack(vp_l), jnp.stack(ks_l), jnp.stack(vs_l),
            jnp.stack(hp_l), jnp.stack(hs_l), jnp.stack(cp_l), jnp.stack(cs_l),
            jnp.stack(pp_l), jnp.stack(ps_l))
```

```python
import functools
import math

import jax
import jax.numpy as jnp
from jax import lax
from jax.experimental import pallas as pl
from jax.experimental.pallas import tpu as pltpu

D_MODEL = 1024
ATT_W = 512
SSD_W = 256
POOL_W = 256
HEAD_DIM = 64
N_HEADS = ATT_W // HEAD_DIM
MOBA_BLOCK = 256
MOBA_TOPK = 3
ATT_SCALE = HEAD_DIM ** -0.5
SSD_HEADDIM = 64
SSD_HEADS = SSD_W // SSD_HEADDIM
SSD_GROUPS = 2
HEADS_PER_GROUP = SSD_HEADS // SSD_GROUPS
D_STATE = 128
SSD_BC = SSD_GROUPS * D_STATE
CONV_W = 4
CONV_DIM = SSD_W + 2 * SSD_BC
SSD_CHUNK = 128
POOL_WINDOWS = (2, 4, 8, 16)
POOL_GROUP_W = POOL_W // len(POOL_WINDOWS)
POOL_HIST = max(POOL_WINDOWS) - 1
PAGE_SIZE = 128
RMS_EPS = 1e-6

LANES = 128
SUBLANES = 8
VMEM_LIMIT_BYTES = 56 * 1024 * 1024

NEG_BIG = -1e30
DT_PAD = LANES
IN_W_PAD = 3 * ATT_W + SSD_W + CONV_DIM + POOL_W + DT_PAD

F32 = jnp.float32
BF16 = jnp.bfloat16
HIGHEST = lax.Precision.HIGHEST

_NT = (((1,), (1,)), ((), ()))
_TN = (((0,), (0,)), ((), ()))


def _silu(x):
    return x / (1.0 + jnp.exp(-x))


def _softplus(x):
    return jnp.maximum(x, 0.0) + jnp.log1p(jnp.exp(-jnp.abs(x)))


def _cparams(*sem):
    return pltpu.CompilerParams(dimension_semantics=sem, vmem_limit_bytes=VMEM_LIMIT_BYTES)


def _const_spec(shape):
    zeros = (0,) * len(shape)
    return pl.BlockSpec(shape, lambda *_: zeros, pipeline_mode=pl.Buffered(1))


def _inproj_kernel(x_ref, g1_ref, w_ref, qg_ref, kg_ref, e_ref,
                   q_ref, k_ref, v_ref, z_ref, xbc_ref, u_ref, dt_ref, *seq_refs, n_blk):
    x = x_ref[...]
    xn = x * lax.rsqrt(jnp.mean(x * x, axis=-1, keepdims=True) + RMS_EPS) * g1_ref[...]
    y = jnp.dot(xn.astype(BF16), w_ref[...], preferred_element_type=F32)

    def head_norm(t, g):
        ms = jnp.dot((t * t).astype(BF16), e_ref[...], preferred_element_type=F32)
        return t * lax.rsqrt(ms + RMS_EPS) * g

    o = 0
    q_ref[...] = head_norm(y[:, o:o + ATT_W], qg_ref[...]); o += ATT_W
    kn = head_norm(y[:, o:o + ATT_W], kg_ref[...]); o += ATT_W
    k_ref[...] = kn
    vv = y[:, o:o + ATT_W]; o += ATT_W
    v_ref[...] = vv
    z_ref[...] = y[:, o:o + SSD_W]; o += SSD_W
    xbc_ref[...] = y[:, o:o + CONV_DIM]; o += CONV_DIM
    u_ref[...] = y[:, o:o + POOL_W]; o += POOL_W
    dt_ref[...] = y[:, o:o + DT_PAD]
    if seq_refs:
        mean_ref, kt_ref, vt_ref = seq_refs
        for b in range(n_blk):
            blk = kn[b * MOBA_BLOCK:(b + 1) * MOBA_BLOCK]
            mean_ref[b] = jnp.sum(blk, axis=0, keepdims=True) * (1.0 / MOBA_BLOCK)
        kt_ref[...] = kn.T
        vt_ref[...] = vv.T


def _inproj(x2d, g1, w_r, qg, kg, e_mat, *, tm, seq_len=None):
    rows = x2d.shape[0]
    assert rows % tm == 0
    n_blk = 0
    row_spec = lambda w: pl.BlockSpec((tm, w), lambda i: (i, 0))
    widths = (ATT_W, ATT_W, ATT_W, SSD_W, CONV_DIM, POOL_W, DT_PAD)
    out_shape = [jax.ShapeDtypeStruct((rows, w), F32) for w in widths]
    out_specs = [row_spec(w) for w in widths]
    if seq_len is not None:
        assert tm % MOBA_BLOCK == 0 and seq_len % tm == 0 and rows % seq_len == 0
        n_blk = tm // MOBA_BLOCK
        tps = seq_len // tm
        out_shape.append(jax.ShapeDtypeStruct((rows // MOBA_BLOCK, 1, ATT_W), F32))
        out_specs.append(pl.BlockSpec((n_blk, 1, ATT_W), lambda i: (i, 0, 0)))
        for _ in range(2):
            out_shape.append(jax.ShapeDtypeStruct((rows // seq_len, ATT_W, seq_len), F32))
            out_specs.append(pl.BlockSpec((None, ATT_W, tm), lambda i: (i // tps, 0, i % tps)))
    return pl.pallas_call(
        functools.partial(_inproj_kernel, n_blk=n_blk),
        grid=(rows // tm,),
        in_specs=[row_spec(D_MODEL), _const_spec((1, D_MODEL)), _const_spec((D_MODEL, IN_W_PAD)),
                  _const_spec((1, ATT_W)), _const_spec((1, ATT_W)), _const_spec((ATT_W, ATT_W))],
        out_specs=out_specs,
        out_shape=out_shape,
        compiler_params=_cparams("arbitrary"),
        name="inproj",
    )(x2d, g1, w_r, qg, kg, e_mat)


def _prompt_attn_kernel(q_ref, k_ref, v_ref, km_ref, o_ref, base_sc, bias_sc, m_sc, l_sc, acc_sc, *, nb):
    j = pl.program_id(1)
    i = pl.program_id(2)
    blk = MOBA_BLOCK
    w2 = 2 * blk

    lane = lax.broadcasted_iota(jnp.int32, (1, w2), 1)
    head = 2 * j + (lane >= blk).astype(jnp.int32)
    slope = jnp.exp2(-(head + 1).astype(F32))

    @pl.when(i == 0)
    def _():
        kl = lax.broadcasted_iota(jnp.int32, (blk, w2), 0)
        ql = lax.broadcasted_iota(jnp.int32, (blk, w2), 1) % blk
        base_sc[...] = -slope * (ql - kl).astype(F32)

    q2 = q_ref[...]
    lane_q = lax.broadcasted_iota(jnp.int32, q2.shape, 1)
    qm = jnp.concatenate([jnp.where(lane_q < HEAD_DIM, q2, 0.0),
                          jnp.where(lane_q >= HEAD_DIM, q2, 0.0)], axis=0)

    gate = lax.dot_general(km_ref[...], qm, _NT, precision=HIGHEST, preferred_element_type=F32)
    bidx = lax.broadcasted_iota(jnp.int32, (nb, w2), 0)
    past = bidx < i
    gate = jnp.where(past, gate, -jnp.inf)
    rank = jnp.zeros((nb, w2), jnp.int32)
    for b in range(nb):
        gb = gate[b:b + 1, :]
        rank = rank + ((gb > gate) | ((gb == gate) & (b < bidx))).astype(jnp.int32)
    sel = past & (rank < MOBA_TOPK)
    bias_sc[...] = jnp.where(sel, -slope * ((i - bidx) * blk).astype(F32), NEG_BIG)

    qs = (qm * ATT_SCALE).astype(BF16)

    def scores(start):
        kb = k_ref[pl.ds(start, blk), :].astype(BF16)
        return lax.dot_general(kb, qs, _NT, preferred_element_type=F32) + base_sc[...]

    def pv(start, p):
        vb = v_ref[pl.ds(start, blk), :].astype(BF16)
        return lax.dot_general(vb, p.astype(BF16), _TN, preferred_element_type=F32)

    own = pl.multiple_of(i * blk, blk)
    kl = lax.broadcasted_iota(jnp.int32, (blk, w2), 0)
    ql = lax.broadcasted_iota(jnp.int32, (blk, w2), 1) % blk
    s = jnp.where(kl <= ql, scores(own), NEG_BIG)
    m0 = jnp.max(s, axis=0, keepdims=True)
    p = jnp.exp(s - m0)
    m_sc[...] = m0
    l_sc[...] = jnp.sum(p, axis=0, keepdims=True)
    acc_sc[...] = pv(own, p)

    def body(b, carry):
        start = pl.multiple_of(b * blk, blk)
        s = scores(start) + bias_sc[pl.ds(b, 1), :]
        m_prev = m_sc[...]
        m_new = jnp.maximum(m_prev, jnp.max(s, axis=0, keepdims=True))
        alpha = jnp.exp(m_prev - m_new)
        p = jnp.exp(s - m_new)
        l_sc[...] = alpha * l_sc[...] + jnp.sum(p, axis=0, keepdims=True)
        acc_sc[...] = alpha * acc_sc[...] + pv(start, p)
        m_sc[...] = m_new
        return carry

    lax.fori_loop(0, i, body, 0)

    o = acc_sc[...] / l_sc[...]
    o2t = jnp.concatenate([o[0:HEAD_DIM, 0:blk], o[HEAD_DIM:2 * HEAD_DIM, blk:w2]], axis=0)
    o_ref[...] = o2t.T


def _prompt_attn(q, k, v, kmean):
    n, t, _ = q.shape
    nb = t // MOBA_BLOCK
    assert t % MOBA_BLOCK == 0 and kmean.shape == (n, nb, ATT_W)
    pairs = ATT_W // LANES
    blk = MOBA_BLOCK
    return pl.pallas_call(
        functools.partial(_prompt_attn_kernel, nb=nb),
        grid=(n, pairs, nb),
        in_specs=[pl.BlockSpec((None, blk, LANES), lambda a, j, i: (a, i, j)),
                  pl.BlockSpec((None, t, LANES), lambda a, j, i: (a, 0, j)),
                  pl.BlockSpec((None, t, LANES), lambda a, j, i: (a, 0, j)),
                  pl.BlockSpec((None, nb, LANES), lambda a, j, i: (a, 0, j))],
        out_specs=pl.BlockSpec((None, blk, LANES), lambda a, j, i: (a, i, j)),
        out_shape=jax.ShapeDtypeStruct((n, t, ATT_W), F32),
        scratch_shapes=[pltpu.VMEM((blk, 2 * blk), F32), pltpu.VMEM((nb, 2 * blk), F32),
                        pltpu.VMEM((1, 2 * blk), F32), pltpu.VMEM((1, 2 * blk), F32),
                        pltpu.VMEM((LANES, 2 * blk), F32)],
        compiler_params=_cparams("arbitrary", "arbitrary", "arbitrary"),
        name="moba_prompt",
    )(q, k, v, kmean)


def _sample_attn_kernel(pt_ref, q_ref, kn_ref, vn_ref, *refs, t, ppc, n_blocks):
    del pt_ref
    kp = refs[:ppc]
    vp = refs[ppc:2 * ppc]
    o_ref = refs[2 * ppc]
    qm_sc, g_sc, m_sc, l_sc, o_sc = refs[2 * ppc + 1:]
    c = pl.program_id(1)
    rows = N_HEADS * t
    bpc = ppc * PAGE_SIZE // MOBA_BLOCK
    ppb = MOBA_BLOCK // PAGE_SIZE
    past_len = n_blocks * MOBA_BLOCK

    rid = lax.broadcasted_iota(jnp.int32, (rows, 1), 0)
    slope = jnp.exp2(-(rid // t + 1).astype(F32))
    posq = (past_len + rid % t).astype(F32)

    @pl.when(c == 0)
    def _():
        qt = q_ref[...]
        r2 = lax.broadcasted_iota(jnp.int32, qt.shape, 0)
        l2 = lax.broadcasted_iota(jnp.int32, qt.shape, 1)
        qm_sc[...] = jnp.where(r2 // t == l2 // HEAD_DIM, qt, 0.0)
        g_sc[...] = jnp.zeros(g_sc.shape, F32)
        m_sc[...] = jnp.full(m_sc.shape, NEG_BIG, F32)
        l_sc[...] = jnp.zeros(l_sc.shape, F32)

    qs = (qm_sc[...] * ATT_SCALE).astype(BF16)
    lane_b = lax.broadcasted_iota(jnp.int32, (rows, LANES), 1)
    kl = lax.broadcasted_iota(jnp.int32, (rows, MOBA_BLOCK), 1).astype(F32)

    for bl in range(bpc):
        gb = c * bpc + bl
        pages = range(bl * ppb, (bl + 1) * ppb)
        s = jnp.concatenate([jnp.dot(qs, kp[r][...].astype(BF16), preferred_element_type=F32)
                             for r in pages], axis=-1)
        gsum = jnp.sum(s, axis=-1, keepdims=True)
        s = s - slope * (posq - (gb * MOBA_BLOCK).astype(F32) - kl)
        mb = jnp.max(s, axis=-1, keepdims=True)
        p = jnp.exp(s - mb)
        lb = jnp.sum(p, axis=-1, keepdims=True)
        ob = None
        for n_, r in enumerate(pages):
            pr = p[:, n_ * PAGE_SIZE:(n_ + 1) * PAGE_SIZE].astype(BF16)
            part = lax.dot_general(pr, vp[r][...].astype(BF16), _NT, preferred_element_type=F32)
            ob = part if ob is None else ob + part
        o_sc[gb] = ob
        g_sc[...] = jnp.where(lane_b == gb, gsum, g_sc[...])
        m_sc[...] = jnp.where(lane_b == gb, mb, m_sc[...])
        l_sc[...] = jnp.where(lane_b == gb, lb, l_sc[...])

    @pl.when(c == pl.num_programs(1) - 1)
    def _():
        valid = lane_b < n_blocks
        gate = jnp.where(valid, g_sc[...], -jnp.inf)
        rank = jnp.zeros((rows, LANES), jnp.int32)
        for b in range(n_blocks):
            gcol = gate[:, b:b + 1]
            rank = rank + ((gcol > gate) | ((gcol == gate) & (b < lane_b))).astype(jnp.int32)
        sel = valid & (rank < MOBA_TOPK)
        kn = kn_ref[...]
        vn = vn_ref[...]
        tp = kn.shape[0]
        s_own = lax.dot_general(qs, kn.astype(BF16), _NT, preferred_element_type=F32)
        tk = lax.broadcasted_iota(jnp.int32, (rows, tp), 1)
        tq = lax.broadcasted_iota(jnp.int32, (rows, tp), 0) % t
        s_own = s_own - slope * (tq - tk).astype(F32)
        s_own = jnp.where(tk <= tq, s_own, NEG_BIG)
        m_blk = jnp.where(sel, m_sc[...], NEG_BIG)
        m = jnp.maximum(jnp.max(m_blk, axis=-1, keepdims=True), jnp.max(s_own, axis=-1, keepdims=True))
        w = jnp.where(sel, jnp.exp(m_blk - m), 0.0)
        p_own = jnp.exp(s_own - m)
        den = jnp.sum(w * l_sc[...], axis=-1, keepdims=True) + jnp.sum(p_own, axis=-1, keepdims=True)
        num = jnp.dot(p_own.astype(BF16), vn.astype(BF16), preferred_element_type=F32)
        for b in range(n_blocks):
            num = num + w[:, b:b + 1] * o_sc[b]
        out = num / den
        r2 = lax.broadcasted_iota(jnp.int32, out.shape, 0)
        l2 = lax.broadcasted_iota(jnp.int32, out.shape, 1)
        out = jnp.where(r2 // t == l2 // HEAD_DIM, out, 0.0)
        pick = (lax.broadcasted_iota(jnp.int32, (tp, rows), 1) % t
                == lax.broadcasted_iota(jnp.int32, (tp, rows), 0)).astype(F32)
        o_ref[...] = jnp.dot(pick, out, precision=HIGHEST, preferred_element_type=F32)


def _cache_pages_t(cache):
    d, n_pool = cache.shape[:2]
    return jnp.transpose(cache, (0, 1, 3, 4, 2)).reshape(d, n_pool, ATT_W, PAGE_SIZE)


def _sample_attn(q, k_new, v_new, ck_t, cv_t, page_table, layer, *, ppc=8):
    n, t, _ = q.shape
    n_pages = page_table.shape[1]
    assert (n_pages * PAGE_SIZE) % MOBA_BLOCK == 0, "cached length must be whole MoBA blocks"
    assert n_pages % ppc == 0 and (ppc * PAGE_SIZE) % MOBA_BLOCK == 0
    n_blocks = n_pages * PAGE_SIZE // MOBA_BLOCK
    assert n_blocks <= LANES
    rows = N_HEADS * t
    tp = -(-t // SUBLANES) * SUBLANES
    q_rep = jnp.tile(q, (1, N_HEADS, 1))
    pad_t = lambda a: jnp.pad(a, ((0, 0), (0, tp - t), (0, 0)))

    def page_spec(r):
        return pl.BlockSpec((None, None, ATT_W, PAGE_SIZE),
                            lambda a, c, pt: (layer, pt[a * n_pages + c * ppc + r], 0, 0))

    tok_spec = lambda r: pl.BlockSpec((None, r, ATT_W), lambda a, c, pt: (a, 0, 0))
    grid_spec = pltpu.PrefetchScalarGridSpec(
        num_scalar_prefetch=1,
        grid=(n, n_pages // ppc),
        in_specs=[tok_spec(rows), tok_spec(tp), tok_spec(tp)] + [page_spec(r) for r in range(ppc)] * 2,
        out_specs=tok_spec(tp),
        scratch_shapes=[pltpu.VMEM((rows, ATT_W), F32), pltpu.VMEM((rows, LANES), F32),
                        pltpu.VMEM((rows, LANES), F32), pltpu.VMEM((rows, LANES), F32),
                        pltpu.VMEM((n_blocks, rows, ATT_W), F32)],
    )
    out = pl.pallas_call(
        functools.partial(_sample_attn_kernel, t=t, ppc=ppc, n_blocks=n_blocks),
        grid_spec=grid_spec,
        out_shape=jax.ShapeDtypeStruct((n, tp, ATT_W), F32),
        compiler_params=_cparams("arbitrary", "arbitrary"),
        name="moba_sample",
    )(page_table.reshape(-1), q_rep, pad_t(k_new), pad_t(v_new), *([ck_t] * ppc), *([cv_t] * ppc))
    return out[:, :t]


_CONV_PAD = SUBLANES
_POOL_PAD = 2 * SUBLANES


def _mixer_kernel(xbc_ref, dt_ref, z_ref, u_ref, cinit_ref, pinit_ref, sinit_ref,
                  cw_ref, cb_ref, dtb_ref, alg_ref, dsk_ref, sg_ref, pw_ref, ps_ref,
                  y_ref, po_ref, sfin_ref,
                  cext, pext, a1, a2, a3, st_sc, *, L, pos0, valid_len):
    c = pl.program_id(1)

    @pl.when(c == 0)
    def _():
        cext[0:_CONV_PAD] = cinit_ref[...]
        pext[0:_POOL_PAD] = pinit_ref[...]
        st_sc[...] = sinit_ref[...].reshape(SSD_W, D_STATE)

    row = lax.broadcasted_iota(jnp.int32, (L, 1), 0)

    cext[_CONV_PAD:_CONV_PAD + L] = xbc_ref[...]
    conv = cb_ref[...] + cext[_CONV_PAD - 3:_CONV_PAD - 3 + L] * cw_ref[0:1, :]
    for i in range(1, CONV_W):
        conv = conv + cext[_CONV_PAD - 3 + i:_CONV_PAD - 3 + i + L] * cw_ref[i:i + 1, :]
    cext[0:_CONV_PAD] = cext[L:L + _CONV_PAD]
    conv = _silu(conv)
    xs = conv[:, 0:SSD_W]
    bm = conv[:, SSD_W:SSD_W + SSD_BC]
    cm = conv[:, SSD_W + SSD_BC:]

    dt = _softplus(dt_ref[...] + dtb_ref[...])
    if valid_len < L:
        dt = jnp.where(row < valid_len, dt, 0.0)
    a = -jnp.exp(alg_ref[...])
    da = dt * a
    ri = lax.broadcasted_iota(jnp.int32, (L, L), 0)
    ci = lax.broadcasted_iota(jnp.int32, (L, L), 1)
    causal = ci <= ri
    tril = causal.astype(F32)
    acum = jnp.dot(tril, da, precision=HIGHEST, preferred_element_type=F32)
    acum_t = acum.T
    a_last = acum[L - 1:L, :]

    lane_head = lax.broadcasted_iota(jnp.int32, (1, SSD_W), 1) // SSD_HEADDIM
    dt_b = jnp.zeros((L, SSD_W), F32)
    for h in range(SSD_HEADS):
        dt_b = jnp.where(lane_head == h, dt[:, h:h + 1], dt_b)
    xd = xs * dt_b
    xd_t = xd.T

    state = st_sc[...]
    y = jnp.zeros((L, SSD_W), F32)
    new_state = []
    for g in range(SSD_GROUPS):
        bg = bm[:, g * D_STATE:(g + 1) * D_STATE]
        cg = cm[:, g * D_STATE:(g + 1) * D_STATE]
        cb = lax.dot_general(cg, bg, _NT, preferred_element_type=F32)
        for hh in range(HEADS_PER_GROUP):
            h = g * HEADS_PER_GROUP + hh
            col = acum[:, h:h + 1]
            diff = col - acum_t[h:h + 1, :]
            lmat = jnp.exp(jnp.where(causal, diff, NEG_BIG))
            y_diag = jnp.dot(cb * lmat, xd, preferred_element_type=F32)
            y_off = lax.dot_general(cg * jnp.exp(col), state, _NT, preferred_element_type=F32)
            y = jnp.where(lane_head == h, y_diag + y_off, y)
            decay = jnp.exp(a_last[:, h:h + 1] - col)
            st = jnp.dot(xd_t, bg * decay, preferred_element_type=F32)
            lo = h * SSD_HEADDIM
            new_state.append(jnp.exp(a_last[:, h:h + 1]) * state[lo:lo + SSD_HEADDIM]
                             + st[lo:lo + SSD_HEADDIM])
    state = jnp.concatenate(new_state, axis=0)
    st_sc[...] = state

    y = y + dsk_ref[...] * xs
    yz = y * _silu(z_ref[...])
    y_ref[...] = yz * lax.rsqrt(jnp.mean(yz * yz, axis=-1, keepdims=True) + RMS_EPS) * sg_ref[...]

    @pl.when(c == pl.num_programs(1) - 1)
    def _():
        sfin_ref[...] = state.reshape(SSD_HEADS, SSD_HEADDIM, D_STATE)

    P = _POOL_PAD
    u = u_ref[...]
    pext[P:P + L] = u
    n_ext = P + L
    a1[1:n_ext] = pext[1:n_ext] + pext[0:n_ext - 1]
    a2[3:n_ext] = a1[3:n_ext] + a1[1:n_ext - 2]
    a3[7:n_ext] = a2[7:n_ext] + a2[3:n_ext - 4]
    s16 = a3[P:n_ext] + a3[P - 8:n_ext - 8]
    pext[0:P] = pext[L:L + P]
    pos1 = pos0 + c * L + row + 1
    lane_grp = lax.broadcasted_iota(jnp.int32, (1, POOL_W), 1) // POOL_GROUP_W
    sums = (a1[P:n_ext], a2[P:n_ext], a3[P:n_ext], s16)
    d = jnp.zeros((L, POOL_W), F32)
    for gi, w in enumerate(POOL_WINDOWS):
        cnt = jnp.minimum(w, pos1).astype(F32)
        d = jnp.where(lane_grp == gi, sums[gi] / cnt, d)
    d = d - u
    po_ref[...] = jnp.dot(d.astype(BF16), pw_ref[...], preferred_element_type=F32) * ps_ref[...]


def _mixer(xbc, dt, z, u, conv_init, pool_init, ssm_init, cw, cb, dtb, alg, dsk, sg, pw_bd, ps,
           *, pos0, valid_len):
    n, t, _ = xbc.shape
    L = SSD_CHUNK
    assert t % L == 0
    valid_len = min(valid_len, L)
    seq = lambda w: pl.BlockSpec((None, L, w), lambda a, c: (a, c, 0))
    per_seq = lambda *s: pl.BlockSpec((None,) + s, lambda a, c: (a,) + (0,) * len(s))
    return pl.pallas_call(
        functools.partial(_mixer_kernel, L=L, pos0=pos0, valid_len=valid_len),
        grid=(n, t // L),
        in_specs=[seq(CONV_DIM), seq(DT_PAD), seq(SSD_W), seq(POOL_W),
                  per_seq(_CONV_PAD, CONV_DIM), per_seq(_POOL_PAD, POOL_W),
                  per_seq(SSD_HEADS, SSD_HEADDIM, D_STATE),
                  _const_spec((CONV_W, CONV_DIM)), _const_spec((1, CONV_DIM)), _const_spec((1, DT_PAD)),
                  _const_spec((1, DT_PAD)), _const_spec((1, SSD_W)), _const_spec((1, SSD_W)),
                  _const_spec((POOL_W, POOL_W)), _const_spec((1, POOL_W))],
        out_specs=[seq(SSD_W), seq(POOL_W), per_seq(SSD_HEADS, SSD_HEADDIM, D_STATE)],
        out_shape=[jax.ShapeDtypeStruct((n, t, SSD_W), F32), jax.ShapeDtypeStruct((n, t, POOL_W), F32),
                   jax.ShapeDtypeStruct((n, SSD_HEADS, SSD_HEADDIM, D_STATE), F32)],
        scratch_shapes=[pltpu.VMEM((_CONV_PAD + L, CONV_DIM), F32), pltpu.VMEM((_POOL_PAD + L, POOL_W), F32),
                        pltpu.VMEM((_POOL_PAD + L, POOL_W), F32), pltpu.VMEM((_POOL_PAD + L, POOL_W), F32),
                        pltpu.VMEM((_POOL_PAD + L, POOL_W), F32), pltpu.VMEM((SSD_W, D_STATE), F32)],
        compiler_params=_cparams("arbitrary", "arbitrary"),
        name="mixer",
    )(xbc, dt, z, u, conv_init, pool_init, ssm_init, cw, cb, dtb, alg, dsk, sg, pw_bd, ps)


_FF_CHUNK = 256


def _outffn_kernel(x_ref, att_ref, y_ref, p_ref, wo_ref, g2_ref, wgu_ref, wdn_ref, o_ref, *, d_ff):
    mix = jnp.concatenate([att_ref[...], y_ref[...], p_ref[...]], axis=-1).astype(BF16)
    h = x_ref[...] + jnp.dot(mix, wo_ref[...], preferred_element_type=F32)
    hn = (h * lax.rsqrt(jnp.mean(h * h, axis=-1, keepdims=True) + RMS_EPS) * g2_ref[...]).astype(BF16)
    acc = h
    for c0 in range(0, d_ff, _FF_CHUNK):
        gate = jnp.dot(hn, wgu_ref[:, c0:c0 + _FF_CHUNK], preferred_element_type=F32)
        up = jnp.dot(hn, wgu_ref[:, d_ff + c0:d_ff + c0 + _FF_CHUNK], preferred_element_type=F32)
        ff = (_silu(gate) * up).astype(BF16)
        acc = acc + jnp.dot(ff, wdn_ref[c0:c0 + _FF_CHUNK, :], preferred_element_type=F32)
    o_ref[...] = acc


def _outffn(x2d, att, y, pool, wo, g2, wgu, wdn, *, tm):
    rows = x2d.shape[0]
    d_ff = wdn.shape[0]
    assert rows % tm == 0 and d_ff % _FF_CHUNK == 0
    row_spec = lambda w: pl.BlockSpec((tm, w), lambda i: (i, 0))
    return pl.pallas_call(
        functools.partial(_outffn_kernel, d_ff=d_ff),
        grid=(rows // tm,),
        in_specs=[row_spec(D_MODEL), row_spec(ATT_W), row_spec(SSD_W), row_spec(POOL_W),
                  _const_spec((D_MODEL, D_MODEL)), _const_spec((1, D_MODEL)),
                  _const_spec((D_MODEL, 2 * d_ff)), _const_spec((d_ff, D_MODEL))],
        out_specs=row_spec(D_MODEL),
        out_shape=jax.ShapeDtypeStruct((rows, D_MODEL), F32),
        compiler_params=_cparams("arbitrary"),
        name="outffn",
    )(x2d, att, y, pool, wo, g2, wgu, wdn)


def _prep_layer_params(norm1, w_in, q_norm, k_norm, conv_w, conv_b, dt_bias, a_log, d_skip, ssd_norm,
                       pool_w, pool_scale, w_out, norm2, w_gu, w_down):
    s = [ATT_W, 2 * ATT_W, 3 * ATT_W, 3 * ATT_W + SSD_W, 3 * ATT_W + SSD_W + CONV_DIM,
         3 * ATT_W + SSD_W + CONV_DIM + SSD_HEADS]
    wq, wk, wv, wz, wx, wdt, wu = jnp.split(w_in, s, axis=-1)
    wdt = jnp.pad(wdt, ((0, 0), (0, DT_PAD - SSD_HEADS)))
    w_r = jnp.concatenate([wq, wk, wv, wz, wx, wu, wdt], axis=-1).astype(BF16)
    pad_h = lambda v: jnp.pad(v.astype(F32), (0, DT_PAD - SSD_HEADS)).reshape(1, DT_PAD)
    pw_bd = jnp.zeros((POOL_W, POOL_W), F32)
    for gi in range(len(POOL_WINDOWS)):
        sl = slice(gi * POOL_GROUP_W, (gi + 1) * POOL_GROUP_W)
        pw_bd = pw_bd.at[sl, sl].set(pool_w[gi].astype(F32))
    return dict(
        g1=norm1.reshape(1, D_MODEL), w_r=w_r,
        qg=jnp.tile(q_norm, N_HEADS).reshape(1, ATT_W), kg=jnp.tile(k_norm, N_HEADS).reshape(1, ATT_W),
        cw=conv_w, cb=conv_b.reshape(1, CONV_DIM), dtb=pad_h(dt_bias), alg=pad_h(a_log),
        dsk=jnp.repeat(d_skip.astype(F32), SSD_HEADDIM).reshape(1, SSD_W), sg=ssd_norm.reshape(1, SSD_W),
        pw_bd=pw_bd.astype(BF16), ps=pool_scale.reshape(1, POOL_W),
        wo=w_out.astype(BF16), g2=norm2.reshape(1, D_MODEL), wgu=w_gu.astype(BF16), wdn=w_down.astype(BF16))


def _head_mean_matrix():
    r = jnp.arange(ATT_W) // HEAD_DIM
    return jnp.where(r[:, None] == r[None, :], 1.0 / HEAD_DIM, 0.0).astype(BF16)


def _layer(x, p, e_mat, conv_prev, ssm_prev, pool_prev, attend, *, pos0, tm, is_prompt):
    n, t, _ = x.shape
    x2d = x.reshape(n * t, D_MODEL)
    outs = _inproj(x2d, p["g1"], p["w_r"], p["qg"], p["kg"], e_mat, tm=tm, seq_len=t if is_prompt else None)
    q, k, v, z, xbc, u, dt = [a.reshape(n, t, a.shape[-1]) for a in outs[:7]]
    if is_prompt:
        kmean = outs[7].reshape(n, t // MOBA_BLOCK, ATT_W)
        k_out, v_out = [jnp.transpose(a.reshape(n, N_HEADS, HEAD_DIM, t), (0, 3, 1, 2)) for a in outs[8:10]]
    else:
        kmean = None
        k_out, v_out = k.reshape(n, t, N_HEADS, HEAD_DIM), v.reshape(n, t, N_HEADS, HEAD_DIM)
    att = attend(q, k, v, kmean)

    tp = -(-t // SSD_CHUNK) * SSD_CHUNK
    pad_t = lambda a: a if tp == t else jnp.pad(a, ((0, 0), (0, tp - t), (0, 0)))
    conv_init = jnp.pad(conv_prev, ((0, 0), (_CONV_PAD - (CONV_W - 1), 0), (0, 0)))
    pool_init = jnp.pad(pool_prev, ((0, 0), (_POOL_PAD - POOL_HIST, 0), (0, 0)))
    y, pool, ssm_new = _mixer(pad_t(xbc), pad_t(dt), pad_t(z), pad_t(u), conv_init, pool_init, ssm_prev,
                              p["cw"], p["cb"], p["dtb"], p["alg"], p["dsk"], p["sg"], p["pw_bd"], p["ps"],
                              pos0=pos0, valid_len=t)
    y = y[:, :t].reshape(n * t, SSD_W)
    pool = pool[:, :t].reshape(n * t, POOL_W)

    out = _outffn(x2d, att.reshape(n * t, ATT_W), y, pool, p["wo"], p["g2"], p["wgu"], p["wdn"], tm=tm)
    conv_state = jnp.concatenate([conv_prev, xbc], axis=1)[:, -(CONV_W - 1):]
    pool_state = jnp.concatenate([pool_prev, u], axis=1)[:, -POOL_HIST:]
    return out.reshape(n, t, D_MODEL), k_out, v_out, conv_state, ssm_new, pool_state


def kernel(x_prompt, x_sample, cache_k, cache_v, page_table, state_ssm, state_conv, state_pool, norm1, w_in, q_norm, k_norm, conv_w, conv_b, dt_bias, a_log, d_skip, ssd_norm, pool_w, pool_scale, w_out, norm2, w_gu, w_down):
    depth = w_in.shape[0]
    b, s, _ = x_prompt.shape
    nd, ts, _ = x_sample.shape
    past_len = page_table.shape[1] * PAGE_SIZE
    e_mat = _head_mean_matrix()
    conv0 = jnp.zeros((b, CONV_W - 1, CONV_DIM), x_prompt.dtype)
    ssm0 = jnp.zeros((b, SSD_HEADS, SSD_HEADDIM, D_STATE), F32)
    pool0 = jnp.zeros((b, POOL_HIST, POOL_W), x_prompt.dtype)
    ck_t, cv_t = _cache_pages_t(cache_k), _cache_pages_t(cache_v)
    yp, ys = x_prompt, x_sample
    acc = [[] for _ in range(10)]
    for l in range(depth):
        p = _prep_layer_params(norm1[l], w_in[l], q_norm[l], k_norm[l], conv_w[l], conv_b[l], dt_bias[l],
                               a_log[l], d_skip[l], ssd_norm[l], pool_w[l], pool_scale[l], w_out[l],
                               norm2[l], w_gu[l], w_down[l])
        attend_p = lambda q, k, v, kmean: _prompt_attn(q, k, v, kmean)
        yp, kp, vp, cp, hp, pp = _layer(yp, p, e_mat, conv0, ssm0, pool0, attend_p,
                                        pos0=0, tm=512, is_prompt=True)
        attend_s = lambda q, k, v, kmean, l=l: _sample_attn(q, k, v, ck_t, cv_t, page_table, l)
        ys, kk, vv, cc, hh, pq = _layer(ys, p, e_mat, state_conv[l], state_ssm[l], state_pool[l], attend_s,
                                        pos0=past_len, tm=nd * ts, is_prompt=False)
        for lst, val in zip(acc, (kp, vp, kk, vv, hp, hh, cp, cc, pp, pq)):
            lst.append(val)
    return (yp, ys) + tuple(jnp.stack(a) for a in acc)
```

```python
import functools
import math

import jax
import jax.numpy as jnp
from jax import lax
from jax.experimental import pallas as pl
from jax.experimental.pallas import tpu as pltpu

D_MODEL = 1024
ATT_W = 512
SSD_W = 256
POOL_W = 256
HEAD_DIM = 64
N_HEADS = ATT_W // HEAD_DIM
MOBA_BLOCK = 256
MOBA_TOPK = 3
ATT_SCALE = HEAD_DIM ** -0.5
SSD_HEADDIM = 64
SSD_HEADS = SSD_W // SSD_HEADDIM
SSD_GROUPS = 2
HEADS_PER_GROUP = SSD_HEADS // SSD_GROUPS
D_STATE = 128
SSD_BC = SSD_GROUPS * D_STATE
CONV_W = 4
CONV_DIM = SSD_W + 2 * SSD_BC
SSD_CHUNK = 128
POOL_WINDOWS = (2, 4, 8, 16)
POOL_GROUP_W = POOL_W // len(POOL_WINDOWS)
POOL_HIST = max(POOL_WINDOWS) - 1
PAGE_SIZE = 128
RMS_EPS = 1e-6

LANES = 128
SUBLANES = 8
VMEM_LIMIT_BYTES = 56 * 1024 * 1024

NEG_BIG = -1e30
DT_PAD = LANES
IN_W_PAD = 3 * ATT_W + SSD_W + CONV_DIM + POOL_W + DT_PAD

F32 = jnp.float32
BF16 = jnp.bfloat16
HIGHEST = lax.Precision.HIGHEST

_NT = (((1,), (1,)), ((), ()))
_TN = (((0,), (0,)), ((), ()))


def _silu(x):
    return x * (0.5 * jnp.tanh(0.5 * x) + 0.5)


def _softplus(x):
    return jnp.maximum(x, 0.0) + jnp.log1p(jnp.exp(-jnp.abs(x)))


def _cparams(*sem):
    return pltpu.CompilerParams(dimension_semantics=sem, vmem_limit_bytes=VMEM_LIMIT_BYTES)


def _const_spec(shape):
    zeros = (0,) * len(shape)
    return pl.BlockSpec(shape, lambda *_: zeros, pipeline_mode=pl.Buffered(1))


def _inproj_kernel(x_ref, g1_ref, wt_ref, qg_ref, kg_ref, e_ref,
                   q_ref, z_ref, xbc_ref, u_ref, dt_ref, *kv_refs, n_blk):
    x = x_ref[...]
    xn = x * lax.rsqrt(jnp.mean(x * x, axis=-1, keepdims=True) + RMS_EPS) * g1_ref[...]
    y = lax.dot_general(xn.astype(BF16), wt_ref[...], _NT, preferred_element_type=F32)

    def head_norm(t, g):
        ms = jnp.dot((t * t).astype(BF16), e_ref[...], preferred_element_type=F32)
        return t * lax.rsqrt(ms + RMS_EPS) * g

    o = 0
    q_ref[...] = head_norm(y[:, o:o + ATT_W], qg_ref[...]); o += ATT_W
    kn = head_norm(y[:, o:o + ATT_W], kg_ref[...]); o += ATT_W
    vv = y[:, o:o + ATT_W]; o += ATT_W
    z_ref[...] = y[:, o:o + SSD_W]; o += SSD_W
    xbc_ref[...] = y[:, o:o + CONV_DIM]; o += CONV_DIM
    u_ref[...] = y[:, o:o + POOL_W]; o += POOL_W
    dt_ref[...] = y[:, o:o + DT_PAD]
    if n_blk == 0:
        k_ref, v_ref = kv_refs
        k_ref[...] = kn
        v_ref[...] = vv
    else:
        kb_ref, vb_ref, kt_ref, vt_ref, mean_ref = kv_refs
        kb_ref[...] = kn.astype(BF16)
        vb_ref[...] = vv.astype(BF16)
        kt_ref[...] = kn.T
        vt_ref[...] = vv.T
        for b in range(n_blk):
            blk = kn[b * MOBA_BLOCK:(b + 1) * MOBA_BLOCK]
            mean_ref[b] = jnp.sum(blk, axis=0, keepdims=True) * (1.0 / MOBA_BLOCK)


def _inproj(x2d, g1, w_t, qg, kg, e_mat, *, tm, seq_len=None):
    rows = x2d.shape[0]
    assert rows % tm == 0
    row_spec = lambda w: pl.BlockSpec((tm, w), lambda i: (i, 0))
    names = ["q", "z", "xbc", "u", "dt"]
    out_shape = [jax.ShapeDtypeStruct((rows, w), F32) for w in (ATT_W, SSD_W, CONV_DIM, POOL_W, DT_PAD)]
    out_specs = [row_spec(s.shape[1]) for s in out_shape]
    if seq_len is None:
        n_blk = 0
        names += ["k", "v"]
        out_shape += [jax.ShapeDtypeStruct((rows, ATT_W), F32)] * 2
        out_specs += [row_spec(ATT_W)] * 2
    else:
        assert tm % MOBA_BLOCK == 0 and seq_len % tm == 0 and rows % seq_len == 0
        n_blk = tm // MOBA_BLOCK
        tps = seq_len // tm
        names += ["k_bf", "v_bf", "kt", "vt", "kmean"]
        out_shape += [jax.ShapeDtypeStruct((rows, ATT_W), BF16)] * 2
        out_specs += [row_spec(ATT_W)] * 2
        out_shape += [jax.ShapeDtypeStruct((rows // seq_len, ATT_W, seq_len), F32)] * 2
        out_specs += [pl.BlockSpec((None, ATT_W, tm), lambda i: (i // tps, 0, i % tps))] * 2
        out_shape.append(jax.ShapeDtypeStruct((rows // MOBA_BLOCK, 1, ATT_W), F32))
        out_specs.append(pl.BlockSpec((n_blk, 1, ATT_W), lambda i: (i, 0, 0)))
    outs = pl.pallas_call(
        functools.partial(_inproj_kernel, n_blk=n_blk),
        grid=(rows // tm,),
        in_specs=[row_spec(D_MODEL), _const_spec((1, D_MODEL)), _const_spec((IN_W_PAD, D_MODEL)),
                  _const_spec((1, ATT_W)), _const_spec((1, ATT_W)), _const_spec((ATT_W, ATT_W))],
        out_specs=out_specs,
        out_shape=out_shape,
        compiler_params=_cparams("arbitrary"),
        name="inproj",
    )(x2d, g1, w_t, qg, kg, e_mat)
    return dict(zip(names, outs))


def _prompt_attn_kernel(q_ref, k_ref, v_ref, km_ref, o_ref,
                        base_sc, bias_sc, qs_sc, s_a, s_b, p_a, p_b, al_a, al_b, m_sc, l_sc, acc_sc, *, nb):
    j = pl.program_id(1)
    i = pl.program_id(2)
    blk = MOBA_BLOCK
    w2 = 2 * blk

    lane = lax.broadcasted_iota(jnp.int32, (1, w2), 1)
    head = 2 * j + (lane >= blk).astype(jnp.int32)
    slope = jnp.exp2(-(head + 1).astype(F32))

    @pl.when(i == 0)
    def _():
        kl = lax.broadcasted_iota(jnp.int32, (blk, w2), 0)
        ql = lax.broadcasted_iota(jnp.int32, (blk, w2), 1) % blk
        base = -slope * (ql - kl).astype(F32)
        base_sc[0] = base
        base_sc[1] = jnp.where(kl <= ql, base, NEG_BIG)

    q2 = q_ref[...]
    lane_q = lax.broadcasted_iota(jnp.int32, q2.shape, 1)
    qm = jnp.concatenate([jnp.where(lane_q < HEAD_DIM, q2, 0.0),
                          jnp.where(lane_q >= HEAD_DIM, q2, 0.0)], axis=0)

    gate = lax.dot_general(km_ref[...], qm, _NT, precision=HIGHEST, preferred_element_type=F32)
    bidx = lax.broadcasted_iota(jnp.int32, (nb, w2), 0)
    past = bidx < i
    gate = jnp.where(past, gate, -jnp.inf)
    rank = jnp.zeros((nb, w2), jnp.int32)
    for b in range(nb):
        gb = gate[b:b + 1, :]
        rank = rank + ((gb > gate) | ((gb == gate) & (b < bidx))).astype(jnp.int32)
    sel = (past & (rank < MOBA_TOPK)) | (bidx == i)
    bias_sc[...] = jnp.where(sel, -slope * ((i - bidx) * blk).astype(F32), NEG_BIG)
    qs_sc[...] = (qm * ATT_SCALE).astype(BF16)

    def qk(b, dst):
        start = pl.multiple_of(b * blk, blk)
        dst[...] = lax.dot_general(k_ref[pl.ds(start, blk), :], qs_sc[...], _NT, preferred_element_type=F32)

    def pv_acc(b, p_ref, al_ref):
        start = pl.multiple_of(b * blk, blk)
        pv = lax.dot_general(v_ref[pl.ds(start, blk), :], p_ref[...], _TN, preferred_element_type=F32)
        acc_sc[...] = al_ref[...] * acc_sc[...] + pv

    def softmax(b, s_ref, p_ref, al_ref):
        s = s_ref[...] + base_sc[(b == i).astype(jnp.int32)]
        r = bias_sc[pl.ds(b, 1), :]
        m_prev = m_sc[...]
        cmax = jnp.max(s, axis=0, keepdims=True)
        m_new = jnp.maximum(m_prev, cmax + r)
        alpha = jnp.exp(m_prev - m_new)
        p = jnp.exp(s - jnp.maximum(m_new - r, cmax))
        l_sc[...] = alpha * l_sc[...] + jnp.sum(p, axis=0, keepdims=True)
        p_ref[...] = p.astype(BF16)
        al_ref[...] = alpha
        m_sc[...] = m_new

    m_sc[...] = jnp.full(m_sc.shape, NEG_BIG, F32)
    l_sc[...] = jnp.zeros(l_sc.shape, F32)
    acc_sc[...] = jnp.zeros(acc_sc.shape, F32)
    p_b[...] = jnp.zeros(p_b.shape, BF16)
    al_b[...] = jnp.ones(al_b.shape, F32)
    qk(0, s_a)

    def body(p, carry):
        a = 2 * p
        softmax(a, s_a, p_a, al_a)
        qk(a + 1, s_b)
        pv_acc(jnp.maximum(a - 1, 0), p_b, al_b)
        softmax(a + 1, s_b, p_b, al_b)
        qk(jnp.minimum(a + 2, nb - 1), s_a)
        pv_acc(a, p_a, al_a)
        return carry

    n_pairs = i // 2 + 1
    lax.fori_loop(0, n_pairs, body, 0)
    pv_acc(2 * n_pairs - 1, p_b, al_b)

    o = acc_sc[...] / l_sc[...]
    o2t = jnp.concatenate([o[0:HEAD_DIM, 0:blk], o[HEAD_DIM:2 * HEAD_DIM, blk:w2]], axis=0)
    o_ref[...] = o2t.T


def _prompt_attn(q, k_bf, v_bf, kmean):
    n, t, _ = q.shape
    nb = t // MOBA_BLOCK
    assert t % MOBA_BLOCK == 0 and nb % 2 == 0 and kmean.shape == (n, nb, ATT_W)
    pairs = ATT_W // LANES
    blk = MOBA_BLOCK
    w2 = 2 * blk
    return pl.pallas_call(
        functools.partial(_prompt_attn_kernel, nb=nb),
        grid=(n, pairs, nb),
        in_specs=[pl.BlockSpec((None, blk, LANES), lambda a, j, i: (a, i, j)),
                  pl.BlockSpec((None, t, LANES), lambda a, j, i: (a, 0, j)),
                  pl.BlockSpec((None, t, LANES), lambda a, j, i: (a, 0, j)),
                  pl.BlockSpec((None, nb, LANES), lambda a, j, i: (a, 0, j))],
        out_specs=pl.BlockSpec((None, blk, LANES), lambda a, j, i: (a, i, j)),
        out_shape=jax.ShapeDtypeStruct((n, t, ATT_W), F32),
        scratch_shapes=[pltpu.VMEM((2, blk, w2), F32), pltpu.VMEM((nb, w2), F32), pltpu.VMEM((w2, LANES), BF16),
                        pltpu.VMEM((blk, w2), F32), pltpu.VMEM((blk, w2), F32),
                        pltpu.VMEM((blk, w2), BF16), pltpu.VMEM((blk, w2), BF16),
                        pltpu.VMEM((1, w2), F32), pltpu.VMEM((1, w2), F32),
                        pltpu.VMEM((1, w2), F32), pltpu.VMEM((1, w2), F32),
                        pltpu.VMEM((LANES, w2), F32)],
        compiler_params=_cparams("arbitrary", "arbitrary", "arbitrary"),
        name="moba_prompt",
    )(q, k_bf, v_bf, kmean)


def _sample_attn_kernel(pt_ref, q_ref, kn_ref, vn_ref, *refs, t, ppc, n_blocks):
    del pt_ref
    kp = refs[:ppc]
    vp = refs[ppc:2 * ppc]
    o_ref = refs[2 * ppc]
    qm_sc, g_sc, m_sc, l_sc, o_sc = refs[2 * ppc + 1:]
    c = pl.program_id(1)
    rows = N_HEADS * t
    bpc = ppc * PAGE_SIZE // MOBA_BLOCK
    ppb = MOBA_BLOCK // PAGE_SIZE
    past_len = n_blocks * MOBA_BLOCK

    rid = lax.broadcasted_iota(jnp.int32, (rows, 1), 0)
    slope = jnp.exp2(-(rid // t + 1).astype(F32))
    posq = (past_len + rid % t).astype(F32)

    @pl.when(c == 0)
    def _():
        qt = q_ref[...]
        r2 = lax.broadcasted_iota(jnp.int32, qt.shape, 0)
        l2 = lax.broadcasted_iota(jnp.int32, qt.shape, 1)
        qm_sc[...] = jnp.where(r2 // t == l2 // HEAD_DIM, qt, 0.0)
        g_sc[...] = jnp.zeros(g_sc.shape, F32)
        m_sc[...] = jnp.full(m_sc.shape, NEG_BIG, F32)
        l_sc[...] = jnp.zeros(l_sc.shape, F32)

    qs = (qm_sc[...] * ATT_SCALE).astype(BF16)
    lane_b = lax.broadcasted_iota(jnp.int32, (rows, LANES), 1)
    kl = lax.broadcasted_iota(jnp.int32, (rows, MOBA_BLOCK), 1).astype(F32)

    g_new, m_new, l_new = g_sc[...], m_sc[...], l_sc[...]
    for bl in range(bpc):
        gb = c * bpc + bl
        pages = range(bl * ppb, (bl + 1) * ppb)
        s = jnp.concatenate([jnp.dot(qs, kp[r][...].astype(BF16), preferred_element_type=F32)
                             for r in pages], axis=-1)
        gsum = jnp.sum(s, axis=-1, keepdims=True)
        s = s - slope * (posq - (gb * MOBA_BLOCK).astype(F32) - kl)
        mb = jnp.max(s, axis=-1, keepdims=True)
        p = jnp.exp(s - mb)
        lb = jnp.sum(p, axis=-1, keepdims=True)
        ob = None
        for n_, r in enumerate(pages):
            pr = p[:, n_ * PAGE_SIZE:(n_ + 1) * PAGE_SIZE].astype(BF16)
            part = lax.dot_general(pr, vp[r][...].astype(BF16), _NT, preferred_element_type=F32)
            ob = part if ob is None else ob + part
        o_sc[gb] = ob
        g_new = jnp.where(lane_b == gb, gsum, g_new)
        m_new = jnp.where(lane_b == gb, mb, m_new)
        l_new = jnp.where(lane_b == gb, lb, l_new)
    g_sc[...] = g_new
    m_sc[...] = m_new
    l_sc[...] = l_new

    @pl.when(c == pl.num_programs(1) - 1)
    def _():
        valid = lane_b < n_blocks
        gate = jnp.where(valid, g_sc[...], -jnp.inf)
        rank = jnp.zeros((rows, LANES), jnp.int32)
        for b in range(n_blocks):
            gcol = gate[:, b:b + 1]
            rank = rank + ((gcol > gate) | ((gcol == gate) & (b < lane_b))).astype(jnp.int32)
        sel = valid & (rank < MOBA_TOPK)
        kn = kn_ref[...]
        vn = vn_ref[...]
        tp = kn.shape[0]
        s_own = lax.dot_general(qs, kn.astype(BF16), _NT, preferred_element_type=F32)
        tk = lax.broadcasted_iota(jnp.int32, (rows, tp), 1)
        tq = lax.broadcasted_iota(jnp.int32, (rows, tp), 0) % t
        s_own = s_own - slope * (tq - tk).astype(F32)
        s_own = jnp.where(tk <= tq, s_own, NEG_BIG)
        m_blk = jnp.where(sel, m_sc[...], NEG_BIG)
        m = jnp.maximum(jnp.max(m_blk, axis=-1, keepdims=True), jnp.max(s_own, axis=-1, keepdims=True))
        w = jnp.where(sel, jnp.exp(m_blk - m), 0.0)
        p_own = jnp.exp(s_own - m)
        den = jnp.sum(w * l_sc[...], axis=-1, keepdims=True) + jnp.sum(p_own, axis=-1, keepdims=True)
        num = jnp.dot(p_own.astype(BF16), vn.astype(BF16), preferred_element_type=F32)
        for b in range(n_blocks):
            num = num + w[:, b:b + 1] * o_sc[b]
        out = num / den
        r2 = lax.broadcasted_iota(jnp.int32, out.shape, 0)
        l2 = lax.broadcasted_iota(jnp.int32, out.shape, 1)
        out = jnp.where(r2 // t == l2 // HEAD_DIM, out, 0.0)
        pick = (lax.broadcasted_iota(jnp.int32, (tp, rows), 1) % t
                == lax.broadcasted_iota(jnp.int32, (tp, rows), 0)).astype(F32)
        o_ref[...] = jnp.dot(pick, out, precision=HIGHEST, preferred_element_type=F32)


def _cache_pages_t(cache):
    d, n_pool = cache.shape[:2]
    return jnp.transpose(cache, (0, 1, 3, 4, 2)).reshape(d, n_pool, ATT_W, PAGE_SIZE)


def _sample_attn(q, k_new, v_new, ck_t, cv_t, page_table, layer, *, ppc=16):
    n, t, _ = q.shape
    n_pages = page_table.shape[1]
    assert (n_pages * PAGE_SIZE) % MOBA_BLOCK == 0, "cached length must be whole MoBA blocks"
    assert n_pages % ppc == 0 and (ppc * PAGE_SIZE) % MOBA_BLOCK == 0
    n_blocks = n_pages * PAGE_SIZE // MOBA_BLOCK
    assert n_blocks <= LANES
    rows = N_HEADS * t
    tp = -(-t // SUBLANES) * SUBLANES
    q_rep = jnp.tile(q, (1, N_HEADS, 1))
    pad_t = lambda a: jnp.pad(a, ((0, 0), (0, tp - t), (0, 0)))

    def page_spec(r):
        return pl.BlockSpec((None, None, ATT_W, PAGE_SIZE),
                            lambda a, c, pt: (layer, pt[a * n_pages + c * ppc + r], 0, 0))

    tok_spec = lambda r: pl.BlockSpec((None, r, ATT_W), lambda a, c, pt: (a, 0, 0))
    grid_spec = pltpu.PrefetchScalarGridSpec(
        num_scalar_prefetch=1,
        grid=(n, n_pages // ppc),
        in_specs=[tok_spec(rows), tok_spec(tp), tok_spec(tp)] + [page_spec(r) for r in range(ppc)] * 2,
        out_specs=tok_spec(tp),
        scratch_shapes=[pltpu.VMEM((rows, ATT_W), F32), pltpu.VMEM((rows, LANES), F32),
                        pltpu.VMEM((rows, LANES), F32), pltpu.VMEM((rows, LANES), F32),
                        pltpu.VMEM((n_blocks, rows, ATT_W), F32)],
    )
    out = pl.pallas_call(
        functools.partial(_sample_attn_kernel, t=t, ppc=ppc, n_blocks=n_blocks),
        grid_spec=grid_spec,
        out_shape=jax.ShapeDtypeStruct((n, tp, ATT_W), F32),
        compiler_params=_cparams("arbitrary", "arbitrary"),
        name="moba_sample",
    )(page_table.reshape(-1), q_rep, pad_t(k_new), pad_t(v_new), *([ck_t] * ppc), *([cv_t] * ppc))
    return out[:, :t]


_CONV_PAD = SUBLANES
_POOL_PAD = 2 * SUBLANES


def _mixer_kernel(xbc_ref, dt_ref, z_ref, u_ref, cinit_ref, pinit_ref, sinit_ref,
                  cw_ref, cb_ref, dtb_ref, alg_ref, dsk_ref, sg_ref, pw_ref, ps_ref,
                  y_ref, po_ref, sfin_ref,
                  cext, pext, a1, a2, a3, st_sc, *, L, pos0, valid_len):
    c = pl.program_id(1)

    @pl.when(c == 0)
    def _():
        cext[0:_CONV_PAD] = cinit_ref[...]
        pext[0:_POOL_PAD] = pinit_ref[...]
        st_sc[...] = sinit_ref[...].reshape(SSD_W, D_STATE)

    row = lax.broadcasted_iota(jnp.int32, (L, 1), 0)

    cext[_CONV_PAD:_CONV_PAD + L] = xbc_ref[...]
    conv = cb_ref[...] + cext[_CONV_PAD - 3:_CONV_PAD - 3 + L] * cw_ref[0:1, :]
    for i in range(1, CONV_W):
        conv = conv + cext[_CONV_PAD - 3 + i:_CONV_PAD - 3 + i + L] * cw_ref[i:i + 1, :]
    cext[0:_CONV_PAD] = cext[L:L + _CONV_PAD]
    conv = _silu(conv)
    xs = conv[:, 0:SSD_W]
    bm = conv[:, SSD_W:SSD_W + SSD_BC]
    cm = conv[:, SSD_W + SSD_BC:]

    dt = _softplus(dt_ref[...] + dtb_ref[...])
    if valid_len < L:
        dt = jnp.where(row < valid_len, dt, 0.0)
    a = -jnp.exp(alg_ref[...])
    da = dt * a
    ri = lax.broadcasted_iota(jnp.int32, (L, L), 0)
    ci = lax.broadcasted_iota(jnp.int32, (L, L), 1)
    causal = ci <= ri
    tril = causal.astype(F32)
    acum = jnp.dot(tril, da, precision=HIGHEST, preferred_element_type=F32)
    acum_t = acum.T
    a_last = acum[L - 1:L, :]

    lane_head = lax.broadcasted_iota(jnp.int32, (1, SSD_W), 1) // SSD_HEADDIM
    dt_b = jnp.zeros((L, SSD_W), F32)
    for h in range(SSD_HEADS):
        dt_b = jnp.where(lane_head == h, dt[:, h:h + 1], dt_b)
    xd = xs * dt_b
    xd_t = xd.T

    state = st_sc[...]
    y = jnp.zeros((L, SSD_W), F32)
    new_state = []
    for g in range(SSD_GROUPS):
        bg = bm[:, g * D_STATE:(g + 1) * D_STATE]
        cg = cm[:, g * D_STATE:(g + 1) * D_STATE]
        cb = lax.dot_general(cg, bg, _NT, preferred_element_type=F32)
        for hh in range(HEADS_PER_GROUP):
            h = g * HEADS_PER_GROUP + hh
            col = acum[:, h:h + 1]
            diff = col - acum_t[h:h + 1, :]
            lmat = jnp.exp(jnp.where(causal, diff, NEG_BIG))
            y_diag = jnp.dot(cb * lmat, xd, preferred_element_type=F32)
            y_off = lax.dot_general(cg * jnp.exp(col), state, _NT, preferred_element_type=F32)
            y = jnp.where(lane_head == h, y_diag + y_off, y)
            decay = jnp.exp(a_last[:, h:h + 1] - col)
            st = jnp.dot(xd_t, bg * decay, preferred_element_type=F32)
            lo = h * SSD_HEADDIM
            new_state.append(jnp.exp(a_last[:, h:h + 1]) * state[lo:lo + SSD_HEADDIM]
                             + st[lo:lo + SSD_HEADDIM])
    state = jnp.concatenate(new_state, axis=0)
    st_sc[...] = state

    y = y + dsk_ref[...] * xs
    yz = y * _silu(z_ref[...])
    y_ref[...] = yz * lax.rsqrt(jnp.mean(yz * yz, axis=-1, keepdims=True) + RMS_EPS) * sg_ref[...]

    @pl.when(c == pl.num_programs(1) - 1)
    def _():
        sfin_ref[...] = state.reshape(SSD_HEADS, SSD_HEADDIM, D_STATE)

    P = _POOL_PAD
    u = u_ref[...]
    pext[P:P + L] = u
    n_ext = P + L
    a1[1:n_ext] = pext[1:n_ext] + pext[0:n_ext - 1]
    a2[3:n_ext] = a1[3:n_ext] + a1[1:n_ext - 2]
    a3[7:n_ext] = a2[7:n_ext] + a2[3:n_ext - 4]
    s16 = a3[P:n_ext] + a3[P - 8:n_ext - 8]
    pext[0:P] = pext[L:L + P]
    pos1 = pos0 + c * L + row + 1
    lane_grp = lax.broadcasted_iota(jnp.int32, (1, POOL_W), 1) // POOL_GROUP_W
    sums = (a1[P:n_ext], a2[P:n_ext], a3[P:n_ext], s16)
    win = s16
    width = jnp.full((1, POOL_W), POOL_WINDOWS[-1], jnp.int32)
    for gi, w in enumerate(POOL_WINDOWS[:-1]):
        win = jnp.where(lane_grp == gi, sums[gi], win)
        width = jnp.where(lane_grp == gi, w, width)
    cnt = jnp.minimum(width, pos1).astype(F32)
    d = win * (1.0 / cnt) - u
    po_ref[...] = jnp.dot(d.astype(BF16), pw_ref[...], preferred_element_type=F32) * ps_ref[...]


def _mixer(xbc, dt, z, u, conv_init, pool_init, ssm_init, cw, cb, dtb, alg, dsk, sg, pw_bd, ps,
           *, pos0, valid_len):
    n, t, _ = xbc.shape
    L = SSD_CHUNK
    assert t % L == 0
    valid_len = min(valid_len, L)
    seq = lambda w: pl.BlockSpec((None, L, w), lambda a, c: (a, c, 0))
    per_seq = lambda *s: pl.BlockSpec((None,) + s, lambda a, c: (a,) + (0,) * len(s))
    return pl.pallas_call(
        functools.partial(_mixer_kernel, L=L, pos0=pos0, valid_len=valid_len),
        grid=(n, t // L),
        in_specs=[seq(CONV_DIM), seq(DT_PAD), seq(SSD_W), seq(POOL_W),
                  per_seq(_CONV_PAD, CONV_DIM), per_seq(_POOL_PAD, POOL_W),
                  per_seq(SSD_HEADS, SSD_HEADDIM, D_STATE),
                  _const_spec((CONV_W, CONV_DIM)), _const_spec((1, CONV_DIM)), _const_spec((1, DT_PAD)),
                  _const_spec((1, DT_PAD)), _const_spec((1, SSD_W)), _const_spec((1, SSD_W)),
                  _const_spec((POOL_W, POOL_W)), _const_spec((1, POOL_W))],
        out_specs=[seq(SSD_W), seq(POOL_W), per_seq(SSD_HEADS, SSD_HEADDIM, D_STATE)],
        out_shape=[jax.ShapeDtypeStruct((n, t, SSD_W), F32), jax.ShapeDtypeStruct((n, t, POOL_W), F32),
                   jax.ShapeDtypeStruct((n, SSD_HEADS, SSD_HEADDIM, D_STATE), F32)],
        scratch_shapes=[pltpu.VMEM((_CONV_PAD + L, CONV_DIM), F32), pltpu.VMEM((_POOL_PAD + L, POOL_W), F32),
                        pltpu.VMEM((_POOL_PAD + L, POOL_W), F32), pltpu.VMEM((_POOL_PAD + L, POOL_W), F32),
                        pltpu.VMEM((_POOL_PAD + L, POOL_W), F32), pltpu.VMEM((SSD_W, D_STATE), F32)],
        compiler_params=_cparams("arbitrary", "arbitrary"),
        name="mixer",
    )(xbc, dt, z, u, conv_init, pool_init, ssm_init, cw, cb, dtb, alg, dsk, sg, pw_bd, ps)


_FF_CHUNK = 256


def _outffn_kernel(x_ref, att_ref, y_ref, p_ref, wo_ref, g2_ref, wgu_ref, wdn_ref, o_ref, *, d_ff):
    mix = jnp.concatenate([att_ref[...], y_ref[...], p_ref[...]], axis=-1).astype(BF16)
    h = x_ref[...] + jnp.dot(mix, wo_ref[...], preferred_element_type=F32)
    hn = (h * lax.rsqrt(jnp.mean(h * h, axis=-1, keepdims=True) + RMS_EPS) * g2_ref[...]).astype(BF16)
    acc = h
    for c0 in range(0, d_ff, _FF_CHUNK):
        gate = jnp.dot(hn, wgu_ref[:, c0:c0 + _FF_CHUNK], preferred_element_type=F32)
        up = jnp.dot(hn, wgu_ref[:, d_ff + c0:d_ff + c0 + _FF_CHUNK], preferred_element_type=F32)
        ff = (_silu(gate) * up).astype(BF16)
        acc = acc + jnp.dot(ff, wdn_ref[c0:c0 + _FF_CHUNK, :], preferred_element_type=F32)
    o_ref[...] = acc


def _outffn(x2d, att, y, pool, wo, g2, wgu, wdn, *, tm):
    rows = x2d.shape[0]
    d_ff = wdn.shape[0]
    assert rows % tm == 0 and d_ff % _FF_CHUNK == 0
    row_spec = lambda w: pl.BlockSpec((tm, w), lambda i: (i, 0))
    return pl.pallas_call(
        functools.partial(_outffn_kernel, d_ff=d_ff),
        grid=(rows // tm,),
        in_specs=[row_spec(D_MODEL), row_spec(ATT_W), row_spec(SSD_W), row_spec(POOL_W),
                  _const_spec((D_MODEL, D_MODEL)), _const_spec((1, D_MODEL)),
                  _const_spec((D_MODEL, 2 * d_ff)), _const_spec((d_ff, D_MODEL))],
        out_specs=row_spec(D_MODEL),
        out_shape=jax.ShapeDtypeStruct((rows, D_MODEL), F32),
        compiler_params=_cparams("arbitrary"),
        name="outffn",
    )(x2d, att, y, pool, wo, g2, wgu, wdn)


def _prep_layer_params(norm1, w_in, q_norm, k_norm, conv_w, conv_b, dt_bias, a_log, d_skip, ssd_norm,
                       pool_w, pool_scale, w_out, norm2, w_gu, w_down):
    s = [ATT_W, 2 * ATT_W, 3 * ATT_W, 3 * ATT_W + SSD_W, 3 * ATT_W + SSD_W + CONV_DIM,
         3 * ATT_W + SSD_W + CONV_DIM + SSD_HEADS]
    wq, wk, wv, wz, wx, wdt, wu = jnp.split(jnp.transpose(w_in), s, axis=0)
    wdt = jnp.pad(wdt, ((0, DT_PAD - SSD_HEADS), (0, 0)))
    w_t = jnp.concatenate([wq, wk, wv, wz, wx, wu, wdt], axis=0).astype(BF16)
    pad_h = lambda v: jnp.pad(v.astype(F32), (0, DT_PAD - SSD_HEADS)).reshape(1, DT_PAD)
    pw_bd = jnp.zeros((POOL_W, POOL_W), F32)
    for gi in range(len(POOL_WINDOWS)):
        sl = slice(gi * POOL_GROUP_W, (gi + 1) * POOL_GROUP_W)
        pw_bd = pw_bd.at[sl, sl].set(pool_w[gi].astype(F32))
    return dict(
        g1=norm1.reshape(1, D_MODEL), w_t=w_t,
        qg=jnp.tile(q_norm, N_HEADS).reshape(1, ATT_W), kg=jnp.tile(k_norm, N_HEADS).reshape(1, ATT_W),
        cw=conv_w, cb=conv_b.reshape(1, CONV_DIM), dtb=pad_h(dt_bias), alg=pad_h(a_log),
        dsk=jnp.repeat(d_skip.astype(F32), SSD_HEADDIM).reshape(1, SSD_W), sg=ssd_norm.reshape(1, SSD_W),
        pw_bd=pw_bd.astype(BF16), ps=pool_scale.reshape(1, POOL_W),
        wo=w_out.astype(BF16), g2=norm2.reshape(1, D_MODEL), wgu=w_gu.astype(BF16), wdn=w_down.astype(BF16))


def _head_mean_matrix():
    r = jnp.arange(ATT_W) // HEAD_DIM
    return jnp.where(r[:, None] == r[None, :], 1.0 / HEAD_DIM, 0.0).astype(BF16)


def _layer(x, p, e_mat, conv_prev, ssm_prev, pool_prev, attend, *, pos0, tm, is_prompt):
    n, t, _ = x.shape
    x2d = x.reshape(n * t, D_MODEL)
    pr = _inproj(x2d, p["g1"], p["w_t"], p["qg"], p["kg"], e_mat, tm=tm, seq_len=t if is_prompt else None)
    seq = lambda a: a.reshape(n, t, a.shape[-1])
    q, z, xbc, u, dt = [seq(pr[name]) for name in ("q", "z", "xbc", "u", "dt")]
    if is_prompt:
        kmean = pr["kmean"].reshape(n, t // MOBA_BLOCK, ATT_W)
        att = attend(q, seq(pr["k_bf"]), seq(pr["v_bf"]), kmean)
        k_out, v_out = [jnp.transpose(pr[name].reshape(n, N_HEADS, HEAD_DIM, t), (0, 3, 1, 2))
                        for name in ("kt", "vt")]
    else:
        att = attend(q, seq(pr["k"]), seq(pr["v"]), None)
        k_out, v_out = [pr[name].reshape(n, t, N_HEADS, HEAD_DIM) for name in ("k", "v")]

    tp = -(-t // SSD_CHUNK) * SSD_CHUNK
    pad_t = lambda a: a if tp == t else jnp.pad(a, ((0, 0), (0, tp - t), (0, 0)))
    conv_init = jnp.pad(conv_prev, ((0, 0), (_CONV_PAD - (CONV_W - 1), 0), (0, 0)))
    pool_init = jnp.pad(pool_prev, ((0, 0), (_POOL_PAD - POOL_HIST, 0), (0, 0)))
    y, pool, ssm_new = _mixer(pad_t(xbc), pad_t(dt), pad_t(z), pad_t(u), conv_init, pool_init, ssm_prev,
                              p["cw"], p["cb"], p["dtb"], p["alg"], p["dsk"], p["sg"], p["pw_bd"], p["ps"],
                              pos0=pos0, valid_len=t)
    y = y[:, :t].reshape(n * t, SSD_W)
    pool = pool[:, :t].reshape(n * t, POOL_W)

    out = _outffn(x2d, att.reshape(n * t, ATT_W), y, pool, p["wo"], p["g2"], p["wgu"], p["wdn"], tm=tm)
    conv_state = jnp.concatenate([conv_prev, xbc], axis=1)[:, -(CONV_W - 1):]
    pool_state = jnp.concatenate([pool_prev, u], axis=1)[:, -POOL_HIST:]
    return out.reshape(n, t, D_MODEL), k_out, v_out, conv_state, ssm_new, pool_state


def kernel(x_prompt, x_sample, cache_k, cache_v, page_table, state_ssm, state_conv, state_pool, norm1, w_in, q_norm, k_norm, conv_w, conv_b, dt_bias, a_log, d_skip, ssd_norm, pool_w, pool_scale, w_out, norm2, w_gu, w_down):
    depth = w_in.shape[0]
    b, s, _ = x_prompt.shape
    nd, ts, _ = x_sample.shape
    past_len = page_table.shape[1] * PAGE_SIZE
    e_mat = _head_mean_matrix()
    conv0 = jnp.zeros((b, CONV_W - 1, CONV_DIM), x_prompt.dtype)
    ssm0 = jnp.zeros((b, SSD_HEADS, SSD_HEADDIM, D_STATE), F32)
    pool0 = jnp.zeros((b, POOL_HIST, POOL_W), x_prompt.dtype)
    ck_t, cv_t = _cache_pages_t(cache_k), _cache_pages_t(cache_v)
    yp, ys = x_prompt, x_sample
    acc = [[] for _ in range(10)]
    for l in range(depth):
        p = _prep_layer_params(norm1[l], w_in[l], q_norm[l], k_norm[l], conv_w[l], conv_b[l], dt_bias[l],
                               a_log[l], d_skip[l], ssd_norm[l], pool_w[l], pool_scale[l], w_out[l],
                               norm2[l], w_gu[l], w_down[l])
        attend_p = lambda q, k, v, kmean: _prompt_attn(q, k, v, kmean)
        yp, kp, vp, cp, hp, pp = _layer(yp, p, e_mat, conv0, ssm0, pool0, attend_p,
                                        pos0=0, tm=512, is_prompt=True)
        attend_s = lambda q, k, v, kmean, l=l: _sample_attn(q, k, v, ck_t, cv_t, page_table, l)
        ys, kk, vv, cc, hh, pq = _layer(ys, p, e_mat, state_conv[l], state_ssm[l], state_pool[l], attend_s,
                                        pos0=past_len, tm=nd * ts, is_prompt=False)
        for lst, val in zip(acc, (kp, vp, kk, vv, hp, hh, cp, cc, pp, pq)):
            lst.append(val)
    return (yp, ys) + tuple(jnp.stack(a) for a in acc)
```

```python
import functools
import math

import jax
import jax.numpy as jnp
from jax import lax
from jax.experimental import pallas as pl
from jax.experimental.pallas import tpu as pltpu

D_MODEL = 1024
ATT_W = 512
SSD_W = 256
POOL_W = 256
HEAD_DIM = 64
N_HEADS = ATT_W // HEAD_DIM
MOBA_BLOCK = 256
MOBA_TOPK = 3
ATT_SCALE = HEAD_DIM ** -0.5
SSD_HEADDIM = 64
SSD_HEADS = SSD_W // SSD_HEADDIM
SSD_GROUPS = 2
HEADS_PER_GROUP = SSD_HEADS // SSD_GROUPS
D_STATE = 128
SSD_BC = SSD_GROUPS * D_STATE
CONV_W = 4
CONV_DIM = SSD_W + 2 * SSD_BC
SSD_CHUNK = 128
POOL_WINDOWS = (2, 4, 8, 16)
POOL_GROUP_W = POOL_W // len(POOL_WINDOWS)
POOL_HIST = max(POOL_WINDOWS) - 1
PAGE_SIZE = 128
RMS_EPS = 1e-6

LANES = 128
SUBLANES = 8
VMEM_LIMIT_BYTES = 56 * 1024 * 1024

NEG_BIG = -1e30
LOG2E = math.log2(math.e)
DT_PAD = LANES
IN_W_PAD = 3 * ATT_W + SSD_W + CONV_DIM + POOL_W + DT_PAD

F32 = jnp.float32
BF16 = jnp.bfloat16
HIGHEST = lax.Precision.HIGHEST

_NT = (((1,), (1,)), ((), ()))
_TN = (((0,), (0,)), ((), ()))


def _silu(x):
    return x * (0.5 * jnp.tanh(0.5 * x) + 0.5)


def _softplus(x):
    return jnp.maximum(x, 0.0) + jnp.log1p(jnp.exp(-jnp.abs(x)))


def _cparams(*sem, **kw):
    return pltpu.CompilerParams(dimension_semantics=sem, vmem_limit_bytes=VMEM_LIMIT_BYTES, **kw)


def _const_spec(shape):
    zeros = (0,) * len(shape)
    return pl.BlockSpec(shape, lambda *_: zeros, pipeline_mode=pl.Buffered(1))


def _inproj_kernel(x_ref, g1_ref, wt_ref, qg_ref, kg_ref, e_ref,
                   q_ref, z_ref, xbc_ref, u_ref, dt_ref, *kv_refs, n_blk):
    x = x_ref[...]
    xn = x * lax.rsqrt(jnp.mean(x * x, axis=-1, keepdims=True) + RMS_EPS) * g1_ref[...]
    y = lax.dot_general(xn.astype(BF16), wt_ref[...], _NT, preferred_element_type=F32)

    def head_norm(t, g):
        ms = jnp.dot((t * t).astype(BF16), e_ref[...], preferred_element_type=F32)
        return t * lax.rsqrt(ms + RMS_EPS) * g

    o = 0
    q_ref[...] = head_norm(y[:, o:o + ATT_W], qg_ref[...]); o += ATT_W
    kn = head_norm(y[:, o:o + ATT_W], kg_ref[...]); o += ATT_W
    vv = y[:, o:o + ATT_W]; o += ATT_W
    z_ref[...] = y[:, o:o + SSD_W]; o += SSD_W
    xbc_ref[...] = y[:, o:o + CONV_DIM]; o += CONV_DIM
    u_ref[...] = y[:, o:o + POOL_W]; o += POOL_W
    dt_ref[...] = y[:, o:o + DT_PAD]
    if n_blk == 0:
        k_ref, v_ref = kv_refs
        k_ref[...] = kn
        v_ref[...] = vv
    else:
        kb_ref, vb_ref, kt_ref, vt_ref, mean_ref = kv_refs
        tm = kn.shape[0]
        lane = lax.broadcasted_iota(jnp.int32, (tm, LANES), 1)
        kl = (lax.broadcasted_iota(jnp.int32, (tm, LANES), 0) % MOBA_BLOCK).astype(F32)
        k_slabs, v_slabs = [], []
        for c in range(0, ATT_W, LANES):
            for half in (0, 1):
                own = (lane // HEAD_DIM) == half
                in_aux = (lane >= _AUX_K_START[half]) & (lane < _AUX_K_START[half] + _AUX_K_LANES)
                k_slabs.append(jnp.where(own, kn[:, c:c + LANES], jnp.where(in_aux, kl, 0.0)).astype(BF16))
                v_slabs.append(jnp.where(own, vv[:, c:c + LANES],
                                         (lane == _ONES_LANE[half]).astype(F32)).astype(BF16))
        kb_ref[...] = jnp.concatenate(k_slabs, axis=-1)
        vb_ref[...] = jnp.concatenate(v_slabs, axis=-1)
        kt_ref[...] = kn.T
        vt_ref[...] = vv.T
        for b in range(n_blk):
            blk = kn[b * MOBA_BLOCK:(b + 1) * MOBA_BLOCK]
            mean_ref[b] = jnp.sum(blk, axis=0, keepdims=True) * (1.0 / MOBA_BLOCK)


def _inproj(x2d, g1, w_t, qg, kg, e_mat, *, tm, seq_len=None):
    rows = x2d.shape[0]
    assert rows % tm == 0
    row_spec = lambda w: pl.BlockSpec((tm, w), lambda i: (i, 0))
    names = ["q", "z", "xbc", "u", "dt"]
    out_shape = [jax.ShapeDtypeStruct((rows, w), F32) for w in (ATT_W, SSD_W, CONV_DIM, POOL_W, DT_PAD)]
    out_specs = [row_spec(s.shape[1]) for s in out_shape]
    if seq_len is None:
        n_blk = 0
        names += ["k", "v"]
        out_shape += [jax.ShapeDtypeStruct((rows, ATT_W), F32)] * 2
        out_specs += [row_spec(ATT_W)] * 2
    else:
        assert tm % MOBA_BLOCK == 0 and seq_len % tm == 0 and rows % seq_len == 0
        n_blk = tm // MOBA_BLOCK
        tps = seq_len // tm
        names += ["k_aug", "v_aug", "kt", "vt", "kmean"]
        out_shape += [jax.ShapeDtypeStruct((rows, 2 * ATT_W), BF16)] * 2
        out_specs += [row_spec(2 * ATT_W)] * 2
        out_shape += [jax.ShapeDtypeStruct((rows // seq_len, ATT_W, seq_len), F32)] * 2
        out_specs += [pl.BlockSpec((None, ATT_W, tm), lambda i: (i // tps, 0, i % tps))] * 2
        out_shape.append(jax.ShapeDtypeStruct((rows // MOBA_BLOCK, 1, ATT_W), F32))
        out_specs.append(pl.BlockSpec((n_blk, 1, ATT_W), lambda i: (i, 0, 0)))
    outs = pl.pallas_call(
        functools.partial(_inproj_kernel, n_blk=n_blk),
        grid=(rows // tm,),
        in_specs=[row_spec(D_MODEL), _const_spec((1, D_MODEL)), _const_spec((IN_W_PAD, D_MODEL)),
                  _const_spec((1, ATT_W)), _const_spec((1, ATT_W)), _const_spec((ATT_W, ATT_W))],
        out_specs=out_specs,
        out_shape=out_shape,
        compiler_params=_cparams("arbitrary"),
        name="inproj",
    )(x2d, g1, w_t, qg, kg, e_mat)
    return dict(zip(names, outs))


_AUX_K_LANES = 3
_AUX_K_START = (HEAD_DIM, 0)
_ONES_LANE = (HEAD_DIM, HEAD_DIM - 1)
_ACC_ROWS = HEAD_DIM + SUBLANES
_ACC_ROW0 = (0, HEAD_DIM - SUBLANES)


def _bf16_terms(x, n):
    terms = []
    for _ in range(n):
        t = x.astype(BF16).astype(F32)
        terms.append(t)
        x = x - t
    return terms


def _prompt_attn_kernel(q_ref, k_ref, v_ref, km_ref, o_ref,
                        bias_sc, qs_sc, s_a, s_b, s_d, p_a, p_b, al_a, al_b, m_sc, acc_sc, *, nb):
    j = pl.program_id(1)
    i = pl.program_id(2)
    blk = MOBA_BLOCK
    w2 = 2 * blk

    lane = lax.broadcasted_iota(jnp.int32, (1, w2), 1)
    head = 2 * j + (lane >= blk).astype(jnp.int32)
    slope2 = jnp.exp2(-(head + 1).astype(F32)) * LOG2E

    q2 = q_ref[...]
    lane_q = lax.broadcasted_iota(jnp.int32, q2.shape, 1)
    for half in (0, 1):
        own = (lane_q // HEAD_DIM) == half
        aux = jnp.zeros(q2.shape, F32)
        slope_h = jnp.exp2(-(2 * j + half + 1).astype(F32)) * LOG2E
        for n_, term in enumerate(_bf16_terms(jnp.full((1, 1), slope_h, F32), _AUX_K_LANES)):
            aux = jnp.where(lane_q == _AUX_K_START[half] + n_, term, aux)
        qs_sc[half] = jnp.where(own, q2 * (ATT_SCALE * LOG2E), aux).astype(BF16)

    def qk(b, dst):
        start = pl.multiple_of(b * blk, blk)
        for half in (0, 1):
            dst[:, half * blk:(half + 1) * blk] = lax.dot_general(
                k_ref[pl.ds(start, blk), half * LANES:(half + 1) * LANES], qs_sc[half], _NT,
                preferred_element_type=F32)

    def softmax(s, r, p_ref, al_ref):
        m_prev = m_sc[...]
        cmax = jnp.max(s, axis=0, keepdims=True)
        m_new = jnp.maximum(m_prev, cmax + r)
        p_ref[...] = jnp.exp2(s - jnp.maximum(m_new - r, cmax)).astype(BF16)
        al_ref[...] = jnp.exp2(m_prev - m_new)
        m_sc[...] = m_new

    qm = jnp.concatenate([jnp.where(lane_q < HEAD_DIM, q2, 0.0),
                          jnp.where(lane_q >= HEAD_DIM, q2, 0.0)], axis=0)
    km_hi, km_lo = [t.astype(BF16) for t in _bf16_terms(km_ref[...], 2)]
    qm_hi, qm_lo = [t.astype(BF16) for t in _bf16_terms(qm, 2)]
    ntdot = lambda a, b: lax.dot_general(a, b, _NT, preferred_element_type=F32)
    gate = ntdot(km_hi, qm_hi) + (ntdot(km_hi, qm_lo) + ntdot(km_lo, qm_hi))

    qk(i, s_d)
    qk(0, s_a)
    m_sc[...] = jnp.full(m_sc.shape, NEG_BIG, F32)
    kl = lax.broadcasted_iota(jnp.int32, (blk, w2), 0)
    ql = lax.broadcasted_iota(jnp.int32, (blk, w2), 1) % blk
    softmax(jnp.where(kl <= ql, s_d[...], NEG_BIG), 0.0, p_b, al_b)

    bidx = lax.broadcasted_iota(jnp.int32, (nb, w2), 0)
    past = bidx < i
    gate = jnp.where(past, gate, -jnp.inf)
    rank = jnp.zeros((nb, w2), jnp.int32)
    for b in range(nb):
        gb = gate[b:b + 1, :]
        rank = rank + ((gb > gate) | ((gb == gate) & (b < bidx))).astype(jnp.int32)
    sel = past & (rank < MOBA_TOPK)
    bias_sc[...] = jnp.where(sel, -slope2 * ((i - bidx) * blk).astype(F32), NEG_BIG)

    def pv_acc(b, p_ref, al_ref):
        start = pl.multiple_of(b * blk, blk)
        for half in (0, 1):
            cols = slice(half * blk, (half + 1) * blk)
            pv = lax.dot_general(v_ref[pl.ds(start, blk), half * LANES:(half + 1) * LANES], p_ref[:, cols], _TN,
                                 preferred_element_type=F32)
            r0 = _ACC_ROW0[half]
            acc_sc[half] = al_ref[:, cols] * acc_sc[half] + pv[r0:r0 + _ACC_ROWS]

    def softmax_past(b, s_ref, p_ref, al_ref):
        softmax(s_ref[...], bias_sc[pl.ds(b, 1), :], p_ref, al_ref)

    acc_sc[...] = jnp.zeros(acc_sc.shape, F32)

    def body(p, carry):
        a = 2 * p
        softmax_past(a, s_a, p_a, al_a)
        qk(a + 1, s_b)
        pv_acc(jnp.where(p == 0, i, a - 1), p_b, al_b)
        softmax_past(a + 1, s_b, p_b, al_b)
        qk(jnp.minimum(a + 2, nb - 1), s_a)
        pv_acc(a, p_a, al_a)
        return carry

    n_pairs = (i + 1) // 2
    lax.fori_loop(0, n_pairs, body, 0)
    pv_acc(jnp.where(n_pairs == 0, i, 2 * n_pairs - 1), p_b, al_b)

    lo, hi = acc_sc[0], acc_sc[1]
    o2t = jnp.concatenate([lo[:HEAD_DIM] / lo[HEAD_DIM:HEAD_DIM + 1],
                           hi[SUBLANES:] / hi[SUBLANES - 1:SUBLANES]], axis=0)
    o_ref[...] = o2t.T


def _prompt_attn(q, k_aug, v_aug, kmean):
    n, t, _ = q.shape
    nb = t // MOBA_BLOCK
    assert t % MOBA_BLOCK == 0 and nb % 2 == 0 and kmean.shape == (n, nb, ATT_W)
    pairs = ATT_W // LANES
    blk = MOBA_BLOCK
    w2 = 2 * blk
    slab = pl.BlockSpec((None, t, 2 * LANES), lambda a, j, i: (a, 0, j))
    return pl.pallas_call(
        functools.partial(_prompt_attn_kernel, nb=nb),
        grid=(n, pairs, nb),
        in_specs=[pl.BlockSpec((None, blk, LANES), lambda a, j, i: (a, i, j)), slab, slab,
                  pl.BlockSpec((None, nb, LANES), lambda a, j, i: (a, 0, j))],
        out_specs=pl.BlockSpec((None, blk, LANES), lambda a, j, i: (a, i, j)),
        out_shape=jax.ShapeDtypeStruct((n, t, ATT_W), F32),
        scratch_shapes=[pltpu.VMEM((nb, w2), F32), pltpu.VMEM((2, blk, LANES), BF16)]
                       + [pltpu.VMEM((blk, w2), F32)] * 3 + [pltpu.VMEM((blk, w2), BF16)] * 2
                       + [pltpu.VMEM((1, w2), F32)] * 3 + [pltpu.VMEM((2, _ACC_ROWS, blk), F32)],
        compiler_params=_cparams("arbitrary", "arbitrary", "arbitrary"),
        name="moba_prompt",
    )(q, k_aug, v_aug, kmean)


def _sample_attn_kernel(pt_ref, q_ref, kn_ref, vn_ref, *refs, t, ppc, n_blocks):
    del pt_ref
    kp = refs[:ppc]
    vp = refs[ppc:2 * ppc]
    o_ref = refs[2 * ppc]
    qm_sc, g_sc, m_sc, l_sc, o_sc = refs[2 * ppc + 1:]
    c = pl.program_id(1)
    rows = N_HEADS * t
    bpc = ppc * PAGE_SIZE // MOBA_BLOCK
    ppb = MOBA_BLOCK // PAGE_SIZE
    past_len = n_blocks * MOBA_BLOCK

    rid = lax.broadcasted_iota(jnp.int32, (rows, 1), 0)
    slope = jnp.exp2(-(rid // t + 1).astype(F32))
    posq = (past_len + rid % t).astype(F32)

    @pl.when(c == 0)
    def _():
        qt = q_ref[...]
        r2 = lax.broadcasted_iota(jnp.int32, qt.shape, 0)
        l2 = lax.broadcasted_iota(jnp.int32, qt.shape, 1)
        qm_sc[...] = jnp.where(r2 // t == l2 // HEAD_DIM, qt, 0.0)
        g_sc[...] = jnp.zeros(g_sc.shape, F32)
        m_sc[...] = jnp.full(m_sc.shape, NEG_BIG, F32)
        l_sc[...] = jnp.zeros(l_sc.shape, F32)

    qs = (qm_sc[...] * ATT_SCALE).astype(BF16)
    lane_b = lax.broadcasted_iota(jnp.int32, (rows, LANES), 1)
    kl = lax.broadcasted_iota(jnp.int32, (rows, MOBA_BLOCK), 1).astype(F32)

    g_new, m_new, l_new = g_sc[...], m_sc[...], l_sc[...]
    for bl in range(bpc):
        gb = c * bpc + bl
        pages = range(bl * ppb, (bl + 1) * ppb)
        s = jnp.concatenate([jnp.dot(qs, kp[r][...].astype(BF16), preferred_element_type=F32)
                             for r in pages], axis=-1)
        gsum = jnp.sum(s, axis=-1, keepdims=True)
        s = s - slope * (posq - (gb * MOBA_BLOCK).astype(F32) - kl)
        mb = jnp.max(s, axis=-1, keepdims=True)
        p = jnp.exp(s - mb)
        lb = jnp.sum(p, axis=-1, keepdims=True)
        ob = None
        for n_, r in enumerate(pages):
            pr = p[:, n_ * PAGE_SIZE:(n_ + 1) * PAGE_SIZE].astype(BF16)
            part = lax.dot_general(pr, vp[r][...].astype(BF16), _NT, preferred_element_type=F32)
            ob = part if ob is None else ob + part
        o_sc[gb] = ob
        g_new = jnp.where(lane_b == gb, gsum, g_new)
        m_new = jnp.where(lane_b == gb, mb, m_new)
        l_new = jnp.where(lane_b == gb, lb, l_new)
    g_sc[...] = g_new
    m_sc[...] = m_new
    l_sc[...] = l_new

    @pl.when(c == pl.num_programs(1) - 1)
    def _():
        valid = lane_b < n_blocks
        gate = jnp.where(valid, g_sc[...], -jnp.inf)
        rank = jnp.zeros((rows, LANES), jnp.int32)
        for b in range(n_blocks):
            gcol = gate[:, b:b + 1]
            rank = rank + ((gcol > gate) | ((gcol == gate) & (b < lane_b))).astype(jnp.int32)
        sel = valid & (rank < MOBA_TOPK)
        kn = kn_ref[...]
        vn = vn_ref[...]
        tp = kn.shape[0]
        s_own = lax.dot_general(qs, kn.astype(BF16), _NT, preferred_element_type=F32)
        tk = lax.broadcasted_iota(jnp.int32, (rows, tp), 1)
        tq = lax.broadcasted_iota(jnp.int32, (rows, tp), 0) % t
        s_own = s_own - slope * (tq - tk).astype(F32)
        s_own = jnp.where(tk <= tq, s_own, NEG_BIG)
        m_blk = jnp.where(sel, m_sc[...], NEG_BIG)
        m = jnp.maximum(jnp.max(m_blk, axis=-1, keepdims=True), jnp.max(s_own, axis=-1, keepdims=True))
        w = jnp.where(sel, jnp.exp(m_blk - m), 0.0)
        p_own = jnp.exp(s_own - m)
        den = jnp.sum(w * l_sc[...], axis=-1, keepdims=True) + jnp.sum(p_own, axis=-1, keepdims=True)
        num = jnp.dot(p_own.astype(BF16), vn.astype(BF16), preferred_element_type=F32)
        for b in range(n_blocks):
            num = num + w[:, b:b + 1] * o_sc[b]
        out = num / den
        r2 = lax.broadcasted_iota(jnp.int32, out.shape, 0)
        l2 = lax.broadcasted_iota(jnp.int32, out.shape, 1)
        out = jnp.where(r2 // t == l2 // HEAD_DIM, out, 0.0)
        pick = (lax.broadcasted_iota(jnp.int32, (tp, rows), 1) % t
                == lax.broadcasted_iota(jnp.int32, (tp, rows), 0)).astype(F32)
        o_ref[...] = jnp.dot(pick, out, precision=HIGHEST, preferred_element_type=F32)


def _cache_pages_t(cache):
    d, n_pool = cache.shape[:2]
    return jnp.transpose(cache, (0, 1, 3, 4, 2)).reshape(d, n_pool, ATT_W, PAGE_SIZE)


def _sample_attn(q, k_new, v_new, ck_t, cv_t, page_table, layer, *, ppc=16):
    n, t, _ = q.shape
    n_pages = page_table.shape[1]
    assert (n_pages * PAGE_SIZE) % MOBA_BLOCK == 0, "cached length must be whole MoBA blocks"
    assert n_pages % ppc == 0 and (ppc * PAGE_SIZE) % MOBA_BLOCK == 0
    n_blocks = n_pages * PAGE_SIZE // MOBA_BLOCK
    assert n_blocks <= LANES
    rows = N_HEADS * t
    tp = -(-t // SUBLANES) * SUBLANES
    q_rep = jnp.tile(q, (1, N_HEADS, 1))
    pad_t = lambda a: jnp.pad(a, ((0, 0), (0, tp - t), (0, 0)))

    def page_spec(r):
        return pl.BlockSpec((None, None, ATT_W, PAGE_SIZE),
                            lambda a, c, pt: (layer, pt[a * n_pages + c * ppc + r], 0, 0))

    tok_spec = lambda r: pl.BlockSpec((None, r, ATT_W), lambda a, c, pt: (a, 0, 0))
    grid_spec = pltpu.PrefetchScalarGridSpec(
        num_scalar_prefetch=1,
        grid=(n, n_pages // ppc),
        in_specs=[tok_spec(rows), tok_spec(tp), tok_spec(tp)] + [page_spec(r) for r in range(ppc)] * 2,
        out_specs=tok_spec(tp),
        scratch_shapes=[pltpu.VMEM((rows, ATT_W), F32), pltpu.VMEM((rows, LANES), F32),
                        pltpu.VMEM((rows, LANES), F32), pltpu.VMEM((rows, LANES), F32),
                        pltpu.VMEM((n_blocks, rows, ATT_W), F32)],
    )
    out = pl.pallas_call(
        functools.partial(_sample_attn_kernel, t=t, ppc=ppc, n_blocks=n_blocks),
        grid_spec=grid_spec,
        out_shape=jax.ShapeDtypeStruct((n, tp, ATT_W), F32),
        compiler_params=_cparams("arbitrary", "arbitrary"),
        name="moba_sample",
    )(page_table.reshape(-1), q_rep, pad_t(k_new), pad_t(v_new), *([ck_t] * ppc), *([cv_t] * ppc))
    return out[:, :t]


_CONV_PAD = SUBLANES
_POOL_PAD = 2 * SUBLANES


def _mixer_kernel(xbc_ref, dt_ref, z_ref, u_ref, cinit_ref, pinit_ref, sinit_ref,
                  cw_ref, cb_ref, dtb_ref, alg_ref, dsk_ref, sg_ref, pw_ref, ps_ref,
                  y_ref, po_ref, sfin_ref,
                  cext, pext, a1, a2, a3, st_sc, *, L, pos0, valid_len):
    c = pl.program_id(1)

    @pl.when(c == 0)
    def _():
        cext[0:_CONV_PAD] = cinit_ref[...]
        pext[0:_POOL_PAD] = pinit_ref[...]
        st_sc[...] = sinit_ref[...].reshape(SSD_W, D_STATE)

    row = lax.broadcasted_iota(jnp.int32, (L, 1), 0)

    cext[_CONV_PAD:_CONV_PAD + L] = xbc_ref[...]
    conv = cb_ref[...] + cext[_CONV_PAD - 3:_CONV_PAD - 3 + L] * cw_ref[0:1, :]
    for i in range(1, CONV_W):
        conv = conv + cext[_CONV_PAD - 3 + i:_CONV_PAD - 3 + i + L] * cw_ref[i:i + 1, :]
    cext[0:_CONV_PAD] = cext[L:L + _CONV_PAD]
    conv = _silu(conv)
    xs = conv[:, 0:SSD_W]
    bm = conv[:, SSD_W:SSD_W + SSD_BC]
    cm = conv[:, SSD_W + SSD_BC:]

    dt = _softplus(dt_ref[...] + dtb_ref[...])
    if valid_len < L:
        dt = jnp.where(row < valid_len, dt, 0.0)
    a = -jnp.exp(alg_ref[...])
    da = dt * a
    ri = lax.broadcasted_iota(jnp.int32, (L, L), 0)
    ci = lax.broadcasted_iota(jnp.int32, (L, L), 1)
    causal = ci <= ri
    tril = causal.astype(F32)
    acum = jnp.dot(tril, da, precision=HIGHEST, preferred_element_type=F32)
    acum_t = acum.T
    a_last = acum[L - 1:L, :]

    lane_head = lax.broadcasted_iota(jnp.int32, (1, SSD_W), 1) // SSD_HEADDIM
    dt_b = jnp.zeros((L, SSD_W), F32)
    for h in range(SSD_HEADS):
        dt_b = jnp.where(lane_head == h, dt[:, h:h + 1], dt_b)
    xd = xs * dt_b
    xd_t = xd.T

    state = st_sc[...]
    y = jnp.zeros((L, SSD_W), F32)
    new_state = []
    for g in range(SSD_GROUPS):
        bg = bm[:, g * D_STATE:(g + 1) * D_STATE]
        cg = cm[:, g * D_STATE:(g + 1) * D_STATE]
        cb = lax.dot_general(cg, bg, _NT, preferred_element_type=F32)
        for hh in range(HEADS_PER_GROUP):
            h = g * HEADS_PER_GROUP + hh
            col = acum[:, h:h + 1]
            diff = col - acum_t[h:h + 1, :]
            lmat = jnp.exp(jnp.where(causal, diff, NEG_BIG))
            y_diag = jnp.dot(cb * lmat, xd, preferred_element_type=F32)
            y_off = lax.dot_general(cg * jnp.exp(col), state, _NT, preferred_element_type=F32)
            y = jnp.where(lane_head == h, y_diag + y_off, y)
            decay = jnp.exp(a_last[:, h:h + 1] - col)
            st = jnp.dot(xd_t, bg * decay, preferred_element_type=F32)
            lo = h * SSD_HEADDIM
            new_state.append(jnp.exp(a_last[:, h:h + 1]) * state[lo:lo + SSD_HEADDIM]
                             + st[lo:lo + SSD_HEADDIM])
    state = jnp.concatenate(new_state, axis=0)
    st_sc[...] = state

    y = y + dsk_ref[...] * xs
    yz = y * _silu(z_ref[...])
    y_ref[...] = yz * lax.rsqrt(jnp.mean(yz * yz, axis=-1, keepdims=True) + RMS_EPS) * sg_ref[...]

    @pl.when(c == pl.num_programs(1) - 1)
    def _():
        sfin_ref[...] = state.reshape(SSD_HEADS, SSD_HEADDIM, D_STATE)

    P = _POOL_PAD
    u = u_ref[...]
    pext[P:P + L] = u
    n_ext = P + L
    a1[1:n_ext] = pext[1:n_ext] + pext[0:n_ext - 1]
    a2[3:n_ext] = a1[3:n_ext] + a1[1:n_ext - 2]
    a3[7:n_ext] = a2[7:n_ext] + a2[3:n_ext - 4]
    s16 = a3[P:n_ext] + a3[P - 8:n_ext - 8]
    pext[0:P] = pext[L:L + P]
    pos1 = pos0 + c * L + row + 1
    lane_grp = lax.broadcasted_iota(jnp.int32, (1, POOL_W), 1) // POOL_GROUP_W
    sums = (a1[P:n_ext], a2[P:n_ext], a3[P:n_ext], s16)
    win = s16
    width = jnp.full((1, POOL_W), POOL_WINDOWS[-1], jnp.int32)
    for gi, w in enumerate(POOL_WINDOWS[:-1]):
        win = jnp.where(lane_grp == gi, sums[gi], win)
        width = jnp.where(lane_grp == gi, w, width)
    cnt = jnp.minimum(width, pos1).astype(F32)
    d = win * (1.0 / cnt) - u
    po_ref[...] = jnp.dot(d.astype(BF16), pw_ref[...], preferred_element_type=F32) * ps_ref[...]


def _mixer(xbc, dt, z, u, conv_init, pool_init, ssm_init, cw, cb, dtb, alg, dsk, sg, pw_bd, ps,
           *, pos0, valid_len):
    n, t, _ = xbc.shape
    L = SSD_CHUNK
    assert t % L == 0
    valid_len = min(valid_len, L)
    seq = lambda w: pl.BlockSpec((None, L, w), lambda a, c: (a, c, 0))
    per_seq = lambda *s: pl.BlockSpec((None,) + s, lambda a, c: (a,) + (0,) * len(s))
    return pl.pallas_call(
        functools.partial(_mixer_kernel, L=L, pos0=pos0, valid_len=valid_len),
        grid=(n, t // L),
        in_specs=[seq(CONV_DIM), seq(DT_PAD), seq(SSD_W), seq(POOL_W),
                  per_seq(_CONV_PAD, CONV_DIM), per_seq(_POOL_PAD, POOL_W),
                  per_seq(SSD_HEADS, SSD_HEADDIM, D_STATE),
                  _const_spec((CONV_W, CONV_DIM)), _const_spec((1, CONV_DIM)), _const_spec((1, DT_PAD)),
                  _const_spec((1, DT_PAD)), _const_spec((1, SSD_W)), _const_spec((1, SSD_W)),
                  _const_spec((POOL_W, POOL_W)), _const_spec((1, POOL_W))],
        out_specs=[seq(SSD_W), seq(POOL_W), per_seq(SSD_HEADS, SSD_HEADDIM, D_STATE)],
        out_shape=[jax.ShapeDtypeStruct((n, t, SSD_W), F32), jax.ShapeDtypeStruct((n, t, POOL_W), F32),
                   jax.ShapeDtypeStruct((n, SSD_HEADS, SSD_HEADDIM, D_STATE), F32)],
        scratch_shapes=[pltpu.VMEM((_CONV_PAD + L, CONV_DIM), F32), pltpu.VMEM((_POOL_PAD + L, POOL_W), F32),
                        pltpu.VMEM((_POOL_PAD + L, POOL_W), F32), pltpu.VMEM((_POOL_PAD + L, POOL_W), F32),
                        pltpu.VMEM((_POOL_PAD + L, POOL_W), F32), pltpu.VMEM((SSD_W, D_STATE), F32)],
        compiler_params=_cparams("arbitrary", "arbitrary"),
        name="mixer",
    )(xbc, dt, z, u, conv_init, pool_init, ssm_init, cw, cb, dtb, alg, dsk, sg, pw_bd, ps)


_FF_CHUNK = 256


def _outffn_kernel(x_ref, att_ref, y_ref, p_ref, wo_ref, g2_ref, wgu_ref, wdn_ref, o_ref, *, d_ff):
    mix = jnp.concatenate([att_ref[...], y_ref[...], p_ref[...]], axis=-1).astype(BF16)
    h = x_ref[...] + jnp.dot(mix, wo_ref[...], preferred_element_type=F32)
    hn = (h * lax.rsqrt(jnp.mean(h * h, axis=-1, keepdims=True) + RMS_EPS) * g2_ref[...]).astype(BF16)
    acc = h
    for c0 in range(0, d_ff, _FF_CHUNK):
        gate = jnp.dot(hn, wgu_ref[:, c0:c0 + _FF_CHUNK], preferred_element_type=F32)
        up = jnp.dot(hn, wgu_ref[:, d_ff + c0:d_ff + c0 + _FF_CHUNK], preferred_element_type=F32)
        ff = (_silu(gate) * up).astype(BF16)
        acc = acc + jnp.dot(ff, wdn_ref[c0:c0 + _FF_CHUNK, :], preferred_element_type=F32)
    o_ref[...] = acc


def _outffn(x2d, att, y, pool, wo, g2, wgu, wdn, *, tm):
    rows = x2d.shape[0]
    d_ff = wdn.shape[0]
    assert rows % tm == 0 and d_ff % _FF_CHUNK == 0
    row_spec = lambda w: pl.BlockSpec((tm, w), lambda i: (i, 0))
    return pl.pallas_call(
        functools.partial(_outffn_kernel, d_ff=d_ff),
        grid=(rows // tm,),
        in_specs=[row_spec(D_MODEL), row_spec(ATT_W), row_spec(SSD_W), row_spec(POOL_W),
                  _const_spec((D_MODEL, D_MODEL)), _const_spec((1, D_MODEL)),
                  _const_spec((D_MODEL, 2 * d_ff)), _const_spec((d_ff, D_MODEL))],
        out_specs=row_spec(D_MODEL),
        out_shape=jax.ShapeDtypeStruct((rows, D_MODEL), F32),
        compiler_params=_cparams("arbitrary"),
        name="outffn",
    )(x2d, att, y, pool, wo, g2, wgu, wdn)


def _prep_layer_params(norm1, w_in, q_norm, k_norm, conv_w, conv_b, dt_bias, a_log, d_skip, ssd_norm,
                       pool_w, pool_scale, w_out, norm2, w_gu, w_down):
    s = [ATT_W, 2 * ATT_W, 3 * ATT_W, 3 * ATT_W + SSD_W, 3 * ATT_W + SSD_W + CONV_DIM,
         3 * ATT_W + SSD_W + CONV_DIM + SSD_HEADS]
    wq, wk, wv, wz, wx, wdt, wu = jnp.split(jnp.transpose(w_in), s, axis=0)
    wdt = jnp.pad(wdt, ((0, DT_PAD - SSD_HEADS), (0, 0)))
    w_t = jnp.concatenate([wq, wk, wv, wz, wx, wu, wdt], axis=0).astype(BF16)
    pad_h = lambda v: jnp.pad(v.astype(F32), (0, DT_PAD - SSD_HEADS)).reshape(1, DT_PAD)
    pw_bd = jnp.zeros((POOL_W, POOL_W), F32)
    for gi in range(len(POOL_WINDOWS)):
        sl = slice(gi * POOL_GROUP_W, (gi + 1) * POOL_GROUP_W)
        pw_bd = pw_bd.at[sl, sl].set(pool_w[gi].astype(F32))
    return dict(
        g1=norm1.reshape(1, D_MODEL), w_t=w_t,
        qg=jnp.tile(q_norm, N_HEADS).reshape(1, ATT_W), kg=jnp.tile(k_norm, N_HEADS).reshape(1, ATT_W),
        cw=conv_w, cb=conv_b.reshape(1, CONV_DIM), dtb=pad_h(dt_bias), alg=pad_h(a_log),
        dsk=jnp.repeat(d_skip.astype(F32), SSD_HEADDIM).reshape(1, SSD_W), sg=ssd_norm.reshape(1, SSD_W),
        pw_bd=pw_bd.astype(BF16), ps=pool_scale.reshape(1, POOL_W),
        wo=w_out.astype(BF16), g2=norm2.reshape(1, D_MODEL), wgu=w_gu.astype(BF16), wdn=w_down.astype(BF16))


def _head_mean_matrix():
    r = jnp.arange(ATT_W) // HEAD_DIM
    return jnp.where(r[:, None] == r[None, :], 1.0 / HEAD_DIM, 0.0).astype(BF16)


def _layer(x, p, e_mat, conv_prev, ssm_prev, pool_prev, attend, *, pos0, tm, is_prompt):
    n, t, _ = x.shape
    x2d = x.reshape(n * t, D_MODEL)
    pr = _inproj(x2d, p["g1"], p["w_t"], p["qg"], p["kg"], e_mat, tm=tm, seq_len=t if is_prompt else None)
    seq = lambda a: a.reshape(n, t, a.shape[-1])
    q, z, xbc, u, dt = [seq(pr[name]) for name in ("q", "z", "xbc", "u", "dt")]
    if is_prompt:
        kmean = pr["kmean"].reshape(n, t // MOBA_BLOCK, ATT_W)
        att = attend(q, seq(pr["k_aug"]), seq(pr["v_aug"]), kmean)
        k_out, v_out = [jnp.transpose(pr[name].reshape(n, N_HEADS, HEAD_DIM, t), (0, 3, 1, 2))
                        for name in ("kt", "vt")]
    else:
        att = attend(q, seq(pr["k"]), seq(pr["v"]), None)
        k_out, v_out = [pr[name].reshape(n, t, N_HEADS, HEAD_DIM) for name in ("k", "v")]

    tp = -(-t // SSD_CHUNK) * SSD_CHUNK
    pad_t = lambda a: a if tp == t else jnp.pad(a, ((0, 0), (0, tp - t), (0, 0)))
    conv_init = jnp.pad(conv_prev, ((0, 0), (_CONV_PAD - (CONV_W - 1), 0), (0, 0)))
    pool_init = jnp.pad(pool_prev, ((0, 0), (_POOL_PAD - POOL_HIST, 0), (0, 0)))
    y, pool, ssm_new = _mixer(pad_t(xbc), pad_t(dt), pad_t(z), pad_t(u), conv_init, pool_init, ssm_prev,
                              p["cw"], p["cb"], p["dtb"], p["alg"], p["dsk"], p["sg"], p["pw_bd"], p["ps"],
                              pos0=pos0, valid_len=t)
    y = y[:, :t].reshape(n * t, SSD_W)
    pool = pool[:, :t].reshape(n * t, POOL_W)

    out = _outffn(x2d, att.reshape(n * t, ATT_W), y, pool, p["wo"], p["g2"], p["wgu"], p["wdn"], tm=tm)
    conv_state = jnp.concatenate([conv_prev, xbc], axis=1)[:, -(CONV_W - 1):]
    pool_state = jnp.concatenate([pool_prev, u], axis=1)[:, -POOL_HIST:]
    return out.reshape(n, t, D_MODEL), k_out, v_out, conv_state, ssm_new, pool_state


def kernel(x_prompt, x_sample, cache_k, cache_v, page_table, state_ssm, state_conv, state_pool, norm1, w_in, q_norm, k_norm, conv_w, conv_b, dt_bias, a_log, d_skip, ssd_norm, pool_w, pool_scale, w_out, norm2, w_gu, w_down):
    depth = w_in.shape[0]
    b, s, _ = x_prompt.shape
    nd, ts, _ = x_sample.shape
    past_len = page_table.shape[1] * PAGE_SIZE
    e_mat = _head_mean_matrix()
    conv0 = jnp.zeros((b, CONV_W - 1, CONV_DIM), x_prompt.dtype)
    ssm0 = jnp.zeros((b, SSD_HEADS, SSD_HEADDIM, D_STATE), F32)
    pool0 = jnp.zeros((b, POOL_HIST, POOL_W), x_prompt.dtype)
    ck_t, cv_t = _cache_pages_t(cache_k), _cache_pages_t(cache_v)
    yp, ys = x_prompt, x_sample
    acc = [[] for _ in range(10)]
    for l in range(depth):
        p = _prep_layer_params(norm1[l], w_in[l], q_norm[l], k_norm[l], conv_w[l], conv_b[l], dt_bias[l],
                               a_log[l], d_skip[l], ssd_norm[l], pool_w[l], pool_scale[l], w_out[l],
                               norm2[l], w_gu[l], w_down[l])
        attend_p = lambda q, k, v, kmean: _prompt_attn(q, k, v, kmean)
        yp, kp, vp, cp, hp, pp = _layer(yp, p, e_mat, conv0, ssm0, pool0, attend_p,
                                        pos0=0, tm=512, is_prompt=True)
        attend_s = lambda q, k, v, kmean, l=l: _sample_attn(q, k, v, ck_t, cv_t, page_table, l)
        ys, kk, vv, cc, hh, pq = _layer(ys, p, e_mat, state_conv[l], state_ssm[l], state_pool[l], attend_s,
                                        pos0=past_len, tm=nd * ts, is_prompt=False)
        for lst, val in zip(acc, (kp, vp, kk, vv, hp, hh, cp, cc, pp, pq)):
            lst.append(val)
    return (yp, ys) + tuple(jnp.stack(a) for a in acc)
```

```python
import functools
import math

import jax
import jax.numpy as jnp
from jax import lax
from jax.experimental import pallas as pl
from jax.experimental.pallas import tpu as pltpu

D_MODEL = 1024
ATT_W = 512
SSD_W = 256
POOL_W = 256
HEAD_DIM = 64
N_HEADS = ATT_W // HEAD_DIM
MOBA_BLOCK = 256
MOBA_TOPK = 3
ATT_SCALE = HEAD_DIM ** -0.5
SSD_HEADDIM = 64
SSD_HEADS = SSD_W // SSD_HEADDIM
SSD_GROUPS = 2
HEADS_PER_GROUP = SSD_HEADS // SSD_GROUPS
D_STATE = 128
SSD_BC = SSD_GROUPS * D_STATE
CONV_W = 4
CONV_DIM = SSD_W + 2 * SSD_BC
SSD_CHUNK = 128
POOL_WINDOWS = (2, 4, 8, 16)
POOL_GROUP_W = POOL_W // len(POOL_WINDOWS)
POOL_HIST = max(POOL_WINDOWS) - 1
PAGE_SIZE = 128
RMS_EPS = 1e-6

LANES = 128
SUBLANES = 8
VMEM_LIMIT_BYTES = 56 * 1024 * 1024

NEG_BIG = -1e30
LOG2E = math.log2(math.e)
DT_PAD = LANES
IN_W_PAD = 3 * ATT_W + SSD_W + CONV_DIM + POOL_W + DT_PAD

F32 = jnp.float32
BF16 = jnp.bfloat16
HIGHEST = lax.Precision.HIGHEST

_NT = (((1,), (1,)), ((), ()))
_TN = (((0,), (0,)), ((), ()))


def _silu(x):
    return x * (0.5 * jnp.tanh(0.5 * x) + 0.5)


def _softplus(x):
    return jnp.maximum(x, 0.0) + jnp.log1p(jnp.exp(-jnp.abs(x)))


def _cparams(*sem, **kw):
    return pltpu.CompilerParams(dimension_semantics=sem, vmem_limit_bytes=VMEM_LIMIT_BYTES, **kw)


def _const_spec(shape):
    zeros = (0,) * len(shape)
    return pl.BlockSpec(shape, lambda *_: zeros, pipeline_mode=pl.Buffered(1))


def _inproj_kernel(x_ref, g1_ref, wt_ref, qg_ref, kg_ref, e_ref,
                   q_ref, z_ref, xbc_ref, u_ref, dt_ref, *kv_refs, n_blk):
    x = x_ref[...]
    xn = x * lax.rsqrt(jnp.mean(x * x, axis=-1, keepdims=True) + RMS_EPS) * g1_ref[...]
    y = lax.dot_general(xn.astype(BF16), wt_ref[...], _NT, preferred_element_type=F32)

    def head_norm(t, g):
        ms = jnp.dot((t * t).astype(BF16), e_ref[...], preferred_element_type=F32)
        return t * lax.rsqrt(ms + RMS_EPS) * g

    o = 0
    q_ref[...] = head_norm(y[:, o:o + ATT_W], qg_ref[...]); o += ATT_W
    kn = head_norm(y[:, o:o + ATT_W], kg_ref[...]); o += ATT_W
    vv = y[:, o:o + ATT_W]; o += ATT_W
    z_ref[...] = y[:, o:o + SSD_W]; o += SSD_W
    xbc_ref[...] = y[:, o:o + CONV_DIM]; o += CONV_DIM
    u_ref[...] = y[:, o:o + POOL_W]; o += POOL_W
    dt_ref[...] = y[:, o:o + DT_PAD]
    if n_blk == 0:
        k_ref, v_ref = kv_refs
        k_ref[...] = kn
        v_ref[...] = vv
    else:
        kb_ref, vb_ref, kt_ref, vt_ref, mean_ref = kv_refs
        tm = kn.shape[0]
        lane = lax.broadcasted_iota(jnp.int32, (tm, LANES), 1)
        kl = (lax.broadcasted_iota(jnp.int32, (tm, LANES), 0) % MOBA_BLOCK).astype(F32)
        k_slabs, v_slabs = [], []
        for c in range(0, ATT_W, LANES):
            for half in (0, 1):
                own = (lane // HEAD_DIM) == half
                in_aux = (lane >= _AUX_K_START[half]) & (lane < _AUX_K_START[half] + _AUX_K_LANES)
                k_slabs.append(jnp.where(own, kn[:, c:c + LANES], jnp.where(in_aux, kl, 0.0)).astype(BF16))
                v_slabs.append(jnp.where(own, vv[:, c:c + LANES],
                                         (lane == _ONES_LANE[half]).astype(F32)).astype(BF16))
        kb_ref[...] = jnp.concatenate(k_slabs, axis=-1)
        vb_ref[...] = jnp.concatenate(v_slabs, axis=-1)
        kt_ref[...] = kn.T
        vt_ref[...] = vv.T
        for b in range(n_blk):
            blk = kn[b * MOBA_BLOCK:(b + 1) * MOBA_BLOCK]
            mean_ref[b] = jnp.sum(blk, axis=0, keepdims=True) * (1.0 / MOBA_BLOCK)


def _inproj(x2d, g1, w_t, qg, kg, e_mat, *, tm, seq_len=None):
    rows = x2d.shape[0]
    assert rows % tm == 0
    row_spec = lambda w: pl.BlockSpec((tm, w), lambda i: (i, 0))
    names = ["q", "z", "xbc", "u", "dt"]
    out_shape = [jax.ShapeDtypeStruct((rows, w), F32) for w in (ATT_W, SSD_W, CONV_DIM, POOL_W, DT_PAD)]
    out_specs = [row_spec(s.shape[1]) for s in out_shape]
    if seq_len is None:
        n_blk = 0
        names += ["k", "v"]
        out_shape += [jax.ShapeDtypeStruct((rows, ATT_W), F32)] * 2
        out_specs += [row_spec(ATT_W)] * 2
    else:
        assert tm % MOBA_BLOCK == 0 and seq_len % tm == 0 and rows % seq_len == 0
        n_blk = tm // MOBA_BLOCK
        tps = seq_len // tm
        names += ["k_aug", "v_aug", "kt", "vt", "kmean"]
        out_shape += [jax.ShapeDtypeStruct((rows, 2 * ATT_W), BF16)] * 2
        out_specs += [row_spec(2 * ATT_W)] * 2
        out_shape += [jax.ShapeDtypeStruct((rows // seq_len, ATT_W, seq_len), F32)] * 2
        out_specs += [pl.BlockSpec((None, ATT_W, tm), lambda i: (i // tps, 0, i % tps))] * 2
        out_shape.append(jax.ShapeDtypeStruct((rows // MOBA_BLOCK, 1, ATT_W), F32))
        out_specs.append(pl.BlockSpec((n_blk, 1, ATT_W), lambda i: (i, 0, 0)))
    outs = pl.pallas_call(
        functools.partial(_inproj_kernel, n_blk=n_blk),
        grid=(rows // tm,),
        in_specs=[row_spec(D_MODEL), _const_spec((1, D_MODEL)), _const_spec((IN_W_PAD, D_MODEL)),
                  _const_spec((1, ATT_W)), _const_spec((1, ATT_W)), _const_spec((ATT_W, ATT_W))],
        out_specs=out_specs,
        out_shape=out_shape,
        compiler_params=_cparams("arbitrary"),
        name="inproj",
    )(x2d, g1, w_t, qg, kg, e_mat)
    return dict(zip(names, outs))


_AUX_K_LANES = 3
_AUX_K_START = (HEAD_DIM, 0)
_ONES_LANE = (HEAD_DIM, HEAD_DIM - 1)
_ACC_ROWS = HEAD_DIM + SUBLANES
_ACC_ROW0 = (0, HEAD_DIM - SUBLANES)


def _bf16_terms(x, n):
    terms = []
    for _ in range(n):
        t = x.astype(BF16).astype(F32)
        terms.append(t)
        x = x - t
    return terms


def _prompt_attn_kernel(q_ref, k_ref, v_ref, km_ref, o_ref,
                        bias_sc, qs_sc, s_a, s_b, s_d, p_a, p_b, al_a, al_b, m_sc, acc_sc, *, nb):
    j = pl.program_id(1)
    i = pl.program_id(2)
    blk = MOBA_BLOCK
    w2 = 2 * blk

    lane = lax.broadcasted_iota(jnp.int32, (1, w2), 1)
    head = 2 * j + (lane >= blk).astype(jnp.int32)
    slope2 = jnp.exp2(-(head + 1).astype(F32)) * LOG2E

    q2 = q_ref[...]
    lane_q = lax.broadcasted_iota(jnp.int32, q2.shape, 1)
    for half in (0, 1):
        own = (lane_q // HEAD_DIM) == half
        aux = jnp.zeros(q2.shape, F32)
        slope_h = jnp.exp2(-(2 * j + half + 1).astype(F32)) * LOG2E
        for n_, term in enumerate(_bf16_terms(jnp.full((1, 1), slope_h, F32), _AUX_K_LANES)):
            aux = jnp.where(lane_q == _AUX_K_START[half] + n_, term, aux)
        qs_sc[half] = jnp.where(own, q2 * (ATT_SCALE * LOG2E), aux).astype(BF16)

    def qk(b, dst):
        start = pl.multiple_of(b * blk, blk)
        for half in (0, 1):
            dst[:, half * blk:(half + 1) * blk] = lax.dot_general(
                k_ref[pl.ds(start, blk), half * LANES:(half + 1) * LANES], qs_sc[half], _NT,
                preferred_element_type=F32)

    def softmax(s, r, p_ref, al_ref):
        m_prev = m_sc[...]
        cmax = jnp.max(s, axis=0, keepdims=True)
        m_new = jnp.maximum(m_prev, cmax + r)
        p_ref[...] = jnp.exp2(s - jnp.maximum(m_new - r, cmax)).astype(BF16)
        al_ref[...] = jnp.exp2(m_prev - m_new)
        m_sc[...] = m_new

    qm = jnp.concatenate([jnp.where(lane_q < HEAD_DIM, q2, 0.0),
                          jnp.where(lane_q >= HEAD_DIM, q2, 0.0)], axis=0)
    km_hi, km_lo = [t.astype(BF16) for t in _bf16_terms(km_ref[...], 2)]
    qm_hi, qm_lo = [t.astype(BF16) for t in _bf16_terms(qm, 2)]
    ntdot = lambda a, b: lax.dot_general(a, b, _NT, preferred_element_type=F32)
    gate = ntdot(km_hi, qm_hi) + (ntdot(km_hi, qm_lo) + ntdot(km_lo, qm_hi))

    qk(i, s_d)
    qk(0, s_a)
    m_sc[...] = jnp.full(m_sc.shape, NEG_BIG, F32)
    kl = lax.broadcasted_iota(jnp.int32, (blk, w2), 0)
    ql = lax.broadcasted_iota(jnp.int32, (blk, w2), 1) % blk
    softmax(jnp.where(kl <= ql, s_d[...], NEG_BIG), 0.0, p_b, al_b)

    bidx = lax.broadcasted_iota(jnp.int32, (nb, w2), 0)
    past = bidx < i
    gate = jnp.where(past, gate, -jnp.inf)
    rank = jnp.zeros((nb, w2), jnp.int32)
    for b in range(nb):
        gb = gate[b:b + 1, :]
        rank = rank + ((gb > gate) | ((gb == gate) & (b < bidx))).astype(jnp.int32)
    sel = past & (rank < MOBA_TOPK)
    bias_sc[...] = jnp.where(sel, -slope2 * ((i - bidx) * blk).astype(F32), NEG_BIG)

    def pv_acc(b, p_ref, al_ref):
        start = pl.multiple_of(b * blk, blk)
        for half in (0, 1):
            cols = slice(half * blk, (half + 1) * blk)
            pv = lax.dot_general(v_ref[pl.ds(start, blk), half * LANES:(half + 1) * LANES], p_ref[:, cols], _TN,
                                 preferred_element_type=F32)
            r0 = _ACC_ROW0[half]
            acc_sc[half] = al_ref[:, cols] * acc_sc[half] + pv[r0:r0 + _ACC_ROWS]

    def softmax_past(b, s_ref, p_ref, al_ref):
        softmax(s_ref[...], bias_sc[pl.ds(b, 1), :], p_ref, al_ref)

    acc_sc[...] = jnp.zeros(acc_sc.shape, F32)

    def body(p, carry):
        a = 2 * p
        softmax_past(a, s_a, p_a, al_a)
        qk(a + 1, s_b)
        pv_acc(jnp.where(p == 0, i, a - 1), p_b, al_b)
        softmax_past(a + 1, s_b, p_b, al_b)
        qk(jnp.minimum(a + 2, nb - 1), s_a)
        pv_acc(a, p_a, al_a)
        return carry

    n_pairs = (i + 1) // 2
    lax.fori_loop(0, n_pairs, body, 0)
    pv_acc(jnp.where(n_pairs == 0, i, 2 * n_pairs - 1), p_b, al_b)

    lo, hi = acc_sc[0], acc_sc[1]
    o2t = jnp.concatenate([lo[:HEAD_DIM] / lo[HEAD_DIM:HEAD_DIM + 1],
                           hi[SUBLANES:] / hi[SUBLANES - 1:SUBLANES]], axis=0)
    o_ref[...] = o2t.T


def _prompt_attn(q, k_aug, v_aug, kmean):
    n, t, _ = q.shape
    nb = t // MOBA_BLOCK
    assert t % MOBA_BLOCK == 0 and nb % 2 == 0 and kmean.shape == (n, nb, ATT_W)
    pairs = ATT_W // LANES
    blk = MOBA_BLOCK
    w2 = 2 * blk
    slab = pl.BlockSpec((None, t, 2 * LANES), lambda a, j, i: (a, 0, j))
    return pl.pallas_call(
        functools.partial(_prompt_attn_kernel, nb=nb),
        grid=(n, pairs, nb),
        in_specs=[pl.BlockSpec((None, blk, LANES), lambda a, j, i: (a, i, j)), slab, slab,
                  pl.BlockSpec((None, nb, LANES), lambda a, j, i: (a, 0, j))],
        out_specs=pl.BlockSpec((None, blk, LANES), lambda a, j, i: (a, i, j)),
        out_shape=jax.ShapeDtypeStruct((n, t, ATT_W), F32),
        scratch_shapes=[pltpu.VMEM((nb, w2), F32), pltpu.VMEM((2, blk, LANES), BF16)]
                       + [pltpu.VMEM((blk, w2), F32)] * 3 + [pltpu.VMEM((blk, w2), BF16)] * 2
                       + [pltpu.VMEM((1, w2), F32)] * 3 + [pltpu.VMEM((2, _ACC_ROWS, blk), F32)],
        compiler_params=_cparams("arbitrary", "arbitrary", "arbitrary"),
        name="moba_prompt",
    )(q, k_aug, v_aug, kmean)


def _sample_attn_kernel(pt_ref, q_ref, kn_ref, vn_ref, *refs, t, ppc, n_blocks):
    del pt_ref
    kp = refs[:ppc]
    vp = refs[ppc:2 * ppc]
    o_ref = refs[2 * ppc]
    qm_sc, g_sc, m_sc, l_sc, o_sc = refs[2 * ppc + 1:]
    c = pl.program_id(1)
    rows = N_HEADS * t
    bpc = ppc * PAGE_SIZE // MOBA_BLOCK
    ppb = MOBA_BLOCK // PAGE_SIZE
    past_len = n_blocks * MOBA_BLOCK

    rid = lax.broadcasted_iota(jnp.int32, (rows, 1), 0)
    slope = jnp.exp2(-(rid // t + 1).astype(F32))
    posq = (past_len + rid % t).astype(F32)

    @pl.when(c == 0)
    def _():
        qt = q_ref[...]
        r2 = lax.broadcasted_iota(jnp.int32, qt.shape, 0)
        l2 = lax.broadcasted_iota(jnp.int32, qt.shape, 1)
        qm_sc[...] = jnp.where(r2 // t == l2 // HEAD_DIM, qt, 0.0)
        g_sc[...] = jnp.zeros(g_sc.shape, F32)
        m_sc[...] = jnp.full(m_sc.shape, NEG_BIG, F32)
        l_sc[...] = jnp.zeros(l_sc.shape, F32)

    qs = (qm_sc[...] * ATT_SCALE).astype(BF16)
    lane_b = lax.broadcasted_iota(jnp.int32, (rows, LANES), 1)
    kl = lax.broadcasted_iota(jnp.int32, (rows, MOBA_BLOCK), 1).astype(F32)

    scores = [jnp.dot(qs, kp[r][...].astype(BF16), preferred_element_type=F32) for r in range(ppc)]
    g_new, m_new, l_new = g_sc[...], m_sc[...], l_sc[...]
    probs = []
    for bl in range(bpc):
        gb = c * bpc + bl
        s = jnp.concatenate(scores[bl * ppb:(bl + 1) * ppb], axis=-1)
        gsum = jnp.sum(s, axis=-1, keepdims=True)
        s = s - slope * (posq - (gb * MOBA_BLOCK).astype(F32) - kl)
        mb = jnp.max(s, axis=-1, keepdims=True)
        p = jnp.exp(s - mb)
        lb = jnp.sum(p, axis=-1, keepdims=True)
        probs.append(p.astype(BF16))
        g_new = jnp.where(lane_b == gb, gsum, g_new)
        m_new = jnp.where(lane_b == gb, mb, m_new)
        l_new = jnp.where(lane_b == gb, lb, l_new)
    g_sc[...] = g_new
    m_sc[...] = m_new
    l_sc[...] = l_new
    for bl in range(bpc):
        ob = None
        for n_ in range(ppb):
            pr = probs[bl][:, n_ * PAGE_SIZE:(n_ + 1) * PAGE_SIZE]
            part = lax.dot_general(pr, vp[bl * ppb + n_][...].astype(BF16), _NT, preferred_element_type=F32)
            ob = part if ob is None else ob + part
        o_sc[c * bpc + bl] = ob

    @pl.when(c == pl.num_programs(1) - 1)
    def _():
        valid = lane_b < n_blocks
        gate = jnp.where(valid, g_sc[...], -jnp.inf)
        rank = jnp.zeros((rows, LANES), jnp.int32)
        for b in range(n_blocks):
            gcol = gate[:, b:b + 1]
            rank = rank + ((gcol > gate) | ((gcol == gate) & (b < lane_b))).astype(jnp.int32)
        sel = valid & (rank < MOBA_TOPK)
        kn = kn_ref[...]
        vn = vn_ref[...]
        tp = kn.shape[0]
        s_own = lax.dot_general(qs, kn.astype(BF16), _NT, preferred_element_type=F32)
        tk = lax.broadcasted_iota(jnp.int32, (rows, tp), 1)
        tq = lax.broadcasted_iota(jnp.int32, (rows, tp), 0) % t
        s_own = s_own - slope * (tq - tk).astype(F32)
        s_own = jnp.where(tk <= tq, s_own, NEG_BIG)
        m_blk = jnp.where(sel, m_sc[...], NEG_BIG)
        m = jnp.maximum(jnp.max(m_blk, axis=-1, keepdims=True), jnp.max(s_own, axis=-1, keepdims=True))
        w = jnp.where(sel, jnp.exp(m_blk - m), 0.0)
        p_own = jnp.exp(s_own - m)
        den = jnp.sum(w * l_sc[...], axis=-1, keepdims=True) + jnp.sum(p_own, axis=-1, keepdims=True)
        num = jnp.dot(p_own.astype(BF16), vn.astype(BF16), preferred_element_type=F32)
        for b in range(n_blocks):
            num = num + w[:, b:b + 1] * o_sc[b]
        out = num / den
        r2 = lax.broadcasted_iota(jnp.int32, out.shape, 0)
        l2 = lax.broadcasted_iota(jnp.int32, out.shape, 1)
        out = jnp.where(r2 // t == l2 // HEAD_DIM, out, 0.0)
        pick = (lax.broadcasted_iota(jnp.int32, (tp, rows), 1) % t
                == lax.broadcasted_iota(jnp.int32, (tp, rows), 0)).astype(F32)
        o_ref[...] = jnp.dot(pick, out, precision=HIGHEST, preferred_element_type=F32)


def _cache_pages_t(cache):
    d, n_pool = cache.shape[:2]
    return jnp.transpose(cache, (0, 1, 3, 4, 2)).reshape(d, n_pool, ATT_W, PAGE_SIZE)


def _sample_attn(q, k_new, v_new, ck_t, cv_t, page_table, layer, *, ppc=16):
    n, t, _ = q.shape
    n_pages = page_table.shape[1]
    assert (n_pages * PAGE_SIZE) % MOBA_BLOCK == 0, "cached length must be whole MoBA blocks"
    assert n_pages % ppc == 0 and (ppc * PAGE_SIZE) % MOBA_BLOCK == 0
    n_blocks = n_pages * PAGE_SIZE // MOBA_BLOCK
    assert n_blocks <= LANES
    rows = N_HEADS * t
    tp = -(-t // SUBLANES) * SUBLANES
    q_rep = jnp.tile(q, (1, N_HEADS, 1))
    pad_t = lambda a: jnp.pad(a, ((0, 0), (0, tp - t), (0, 0)))

    def page_spec(r):
        return pl.BlockSpec((None, None, ATT_W, PAGE_SIZE),
                            lambda a, c, pt: (layer, pt[a * n_pages + c * ppc + r], 0, 0))

    tok_spec = lambda r: pl.BlockSpec((None, r, ATT_W), lambda a, c, pt: (a, 0, 0))
    grid_spec = pltpu.PrefetchScalarGridSpec(
        num_scalar_prefetch=1,
        grid=(n, n_pages // ppc),
        in_specs=[tok_spec(rows), tok_spec(tp), tok_spec(tp)] + [page_spec(r) for r in range(ppc)] * 2,
        out_specs=tok_spec(tp),
        scratch_shapes=[pltpu.VMEM((rows, ATT_W), F32), pltpu.VMEM((rows, LANES), F32),
                        pltpu.VMEM((rows, LANES), F32), pltpu.VMEM((rows, LANES), F32),
                        pltpu.VMEM((n_blocks, rows, ATT_W), F32)],
    )
    out = pl.pallas_call(
        functools.partial(_sample_attn_kernel, t=t, ppc=ppc, n_blocks=n_blocks),
        grid_spec=grid_spec,
        out_shape=jax.ShapeDtypeStruct((n, tp, ATT_W), F32),
        compiler_params=_cparams("arbitrary", "arbitrary"),
        name="moba_sample",
    )(page_table.reshape(-1), q_rep, pad_t(k_new), pad_t(v_new), *([ck_t] * ppc), *([cv_t] * ppc))
    return out[:, :t]


_CONV_PAD = SUBLANES
_POOL_PAD = 2 * SUBLANES


def _mixer_kernel(xbc_ref, dt_ref, z_ref, u_ref, cinit_ref, pinit_ref, sinit_ref,
                  cw_ref, cb_ref, dtb_ref, alg_ref, dsk_ref, sg_ref, pw_ref, ps_ref,
                  y_ref, po_ref, sfin_ref,
                  cext, pext, a1, a2, a3, st_sc, *, L, pos0, valid_len):
    c = pl.program_id(1)

    @pl.when(c == 0)
    def _():
        cext[0:_CONV_PAD] = cinit_ref[...]
        pext[0:_POOL_PAD] = pinit_ref[...]
        st_sc[...] = sinit_ref[...].reshape(SSD_W, D_STATE)

    row = lax.broadcasted_iota(jnp.int32, (L, 1), 0)

    cext[_CONV_PAD:_CONV_PAD + L] = xbc_ref[...]
    conv = cb_ref[...] + cext[_CONV_PAD - 3:_CONV_PAD - 3 + L] * cw_ref[0:1, :]
    for i in range(1, CONV_W):
        conv = conv + cext[_CONV_PAD - 3 + i:_CONV_PAD - 3 + i + L] * cw_ref[i:i + 1, :]
    cext[0:_CONV_PAD] = cext[L:L + _CONV_PAD]
    conv = _silu(conv)
    xs = conv[:, 0:SSD_W]
    bm = conv[:, SSD_W:SSD_W + SSD_BC]
    cm = conv[:, SSD_W + SSD_BC:]

    dt = _softplus(dt_ref[...] + dtb_ref[...])
    if valid_len < L:
        dt = jnp.where(row < valid_len, dt, 0.0)
    a = -jnp.exp(alg_ref[...])
    da = dt * a
    ri = lax.broadcasted_iota(jnp.int32, (L, L), 0)
    ci = lax.broadcasted_iota(jnp.int32, (L, L), 1)
    causal = ci <= ri
    tril = causal.astype(F32)
    acum = jnp.dot(tril, da, precision=HIGHEST, preferred_element_type=F32)
    acum_t = acum.T
    a_last = acum[L - 1:L, :]

    lane_head = lax.broadcasted_iota(jnp.int32, (1, SSD_W), 1) // SSD_HEADDIM
    dt_b = jnp.zeros((L, SSD_W), F32)
    for h in range(SSD_HEADS):
        dt_b = jnp.where(lane_head == h, dt[:, h:h + 1], dt_b)
    xd = xs * dt_b
    xd_t = xd.T

    state = st_sc[...]
    y = jnp.zeros((L, SSD_W), F32)
    new_state = []
    for g in range(SSD_GROUPS):
        bg = bm[:, g * D_STATE:(g + 1) * D_STATE]
        cg = cm[:, g * D_STATE:(g + 1) * D_STATE]
        cb = lax.dot_general(cg, bg, _NT, preferred_element_type=F32)
        for hh in range(HEADS_PER_GROUP):
            h = g * HEADS_PER_GROUP + hh
            col = acum[:, h:h + 1]
            diff = col - acum_t[h:h + 1, :]
            lmat = jnp.exp(jnp.where(causal, diff, NEG_BIG))
            y_diag = jnp.dot(cb * lmat, xd, preferred_element_type=F32)
            y_off = lax.dot_general(cg * jnp.exp(col), state, _NT, preferred_element_type=F32)
            y = jnp.where(lane_head == h, y_diag + y_off, y)
            decay = jnp.exp(a_last[:, h:h + 1] - col)
            st = jnp.dot(xd_t, bg * decay, preferred_element_type=F32)
            lo = h * SSD_HEADDIM
            new_state.append(jnp.exp(a_last[:, h:h + 1]) * state[lo:lo + SSD_HEADDIM]
                             + st[lo:lo + SSD_HEADDIM])
    state = jnp.concatenate(new_state, axis=0)
    st_sc[...] = state

    y = y + dsk_ref[...] * xs
    yz = y * _silu(z_ref[...])
    y_ref[...] = yz * lax.rsqrt(jnp.mean(yz * yz, axis=-1, keepdims=True) + RMS_EPS) * sg_ref[...]

    @pl.when(c == pl.num_programs(1) - 1)
    def _():
        sfin_ref[...] = state.reshape(SSD_HEADS, SSD_HEADDIM, D_STATE)

    P = _POOL_PAD
    u = u_ref[...]
    pext[P:P + L] = u
    n_ext = P + L
    a1[1:n_ext] = pext[1:n_ext] + pext[0:n_ext - 1]
    a2[3:n_ext] = a1[3:n_ext] + a1[1:n_ext - 2]
    a3[7:n_ext] = a2[7:n_ext] + a2[3:n_ext - 4]
    s16 = a3[P:n_ext] + a3[P - 8:n_ext - 8]
    pext[0:P] = pext[L:L + P]
    pos1 = pos0 + c * L + row + 1
    lane_grp = lax.broadcasted_iota(jnp.int32, (1, POOL_W), 1) // POOL_GROUP_W
    sums = (a1[P:n_ext], a2[P:n_ext], a3[P:n_ext], s16)
    win = s16
    width = jnp.full((1, POOL_W), POOL_WINDOWS[-1], jnp.int32)
    for gi, w in enumerate(POOL_WINDOWS[:-1]):
        win = jnp.where(lane_grp == gi, sums[gi], win)
        width = jnp.where(lane_grp == gi, w, width)
    cnt = jnp.minimum(width, pos1).astype(F32)
    d = win * (1.0 / cnt) - u
    po_ref[...] = jnp.dot(d.astype(BF16), pw_ref[...], preferred_element_type=F32) * ps_ref[...]


def _mixer(xbc, dt, z, u, conv_init, pool_init, ssm_init, cw, cb, dtb, alg, dsk, sg, pw_bd, ps,
           *, pos0, valid_len):
    n, t, _ = xbc.shape
    L = SSD_CHUNK
    assert t % L == 0
    valid_len = min(valid_len, L)
    seq = lambda w: pl.BlockSpec((None, L, w), lambda a, c: (a, c, 0))
    per_seq = lambda *s: pl.BlockSpec((None,) + s, lambda a, c: (a,) + (0,) * len(s))
    return pl.pallas_call(
        functools.partial(_mixer_kernel, L=L, pos0=pos0, valid_len=valid_len),
        grid=(n, t // L),
        in_specs=[seq(CONV_DIM), seq(DT_PAD), seq(SSD_W), seq(POOL_W),
                  per_seq(_CONV_PAD, CONV_DIM), per_seq(_POOL_PAD, POOL_W),
                  per_seq(SSD_HEADS, SSD_HEADDIM, D_STATE),
                  _const_spec((CONV_W, CONV_DIM)), _const_spec((1, CONV_DIM)), _const_spec((1, DT_PAD)),
                  _const_spec((1, DT_PAD)), _const_spec((1, SSD_W)), _const_spec((1, SSD_W)),
                  _const_spec((POOL_W, POOL_W)), _const_spec((1, POOL_W))],
        out_specs=[seq(SSD_W), seq(POOL_W), per_seq(SSD_HEADS, SSD_HEADDIM, D_STATE)],
        out_shape=[jax.ShapeDtypeStruct((n, t, SSD_W), F32), jax.ShapeDtypeStruct((n, t, POOL_W), F32),
                   jax.ShapeDtypeStruct((n, SSD_HEADS, SSD_HEADDIM, D_STATE), F32)],
        scratch_shapes=[pltpu.VMEM((_CONV_PAD + L, CONV_DIM), F32), pltpu.VMEM((_POOL_PAD + L, POOL_W), F32),
                        pltpu.VMEM((_POOL_PAD + L, POOL_W), F32), pltpu.VMEM((_POOL_PAD + L, POOL_W), F32),
                        pltpu.VMEM((_POOL_PAD + L, POOL_W), F32), pltpu.VMEM((SSD_W, D_STATE), F32)],
        compiler_params=_cparams("arbitrary", "arbitrary"),
        name="mixer",
    )(xbc, dt, z, u, conv_init, pool_init, ssm_init, cw, cb, dtb, alg, dsk, sg, pw_bd, ps)


_FF_CHUNK = 256


def _outffn_kernel(x_ref, att_ref, y_ref, p_ref, wo_ref, g2_ref, wgu_ref, wdn_ref, o_ref, *, d_ff):
    mix = jnp.concatenate([att_ref[...], y_ref[...], p_ref[...]], axis=-1).astype(BF16)
    h = x_ref[...] + jnp.dot(mix, wo_ref[...], preferred_element_type=F32)
    hn = (h * lax.rsqrt(jnp.mean(h * h, axis=-1, keepdims=True) + RMS_EPS) * g2_ref[...]).astype(BF16)
    acc = h
    for c0 in range(0, d_ff, _FF_CHUNK):
        gate = jnp.dot(hn, wgu_ref[:, c0:c0 + _FF_CHUNK], preferred_element_type=F32)
        up = jnp.dot(hn, wgu_ref[:, d_ff + c0:d_ff + c0 + _FF_CHUNK], preferred_element_type=F32)
        ff = (_silu(gate) * up).astype(BF16)
        acc = acc + jnp.dot(ff, wdn_ref[c0:c0 + _FF_CHUNK, :], preferred_element_type=F32)
    o_ref[...] = acc


def _outffn(x2d, att, y, pool, wo, g2, wgu, wdn, *, tm):
    rows = x2d.shape[0]
    d_ff = wdn.shape[0]
    assert rows % tm == 0 and d_ff % _FF_CHUNK == 0
    row_spec = lambda w: pl.BlockSpec((tm, w), lambda i: (i, 0))
    return pl.pallas_call(
        functools.partial(_outffn_kernel, d_ff=d_ff),
        grid=(rows // tm,),
        in_specs=[row_spec(D_MODEL), row_spec(ATT_W), row_spec(SSD_W), row_spec(POOL_W),
                  _const_spec((D_MODEL, D_MODEL)), _const_spec((1, D_MODEL)),
                  _const_spec((D_MODEL, 2 * d_ff)), _const_spec((d_ff, D_MODEL))],
        out_specs=row_spec(D_MODEL),
        out_shape=jax.ShapeDtypeStruct((rows, D_MODEL), F32),
        compiler_params=_cparams("arbitrary"),
        name="outffn",
    )(x2d, att, y, pool, wo, g2, wgu, wdn)


def _prep_layer_params(norm1, w_in, q_norm, k_norm, conv_w, conv_b, dt_bias, a_log, d_skip, ssd_norm,
                       pool_w, pool_scale, w_out, norm2, w_gu, w_down):
    s = [ATT_W, 2 * ATT_W, 3 * ATT_W, 3 * ATT_W + SSD_W, 3 * ATT_W + SSD_W + CONV_DIM,
         3 * ATT_W + SSD_W + CONV_DIM + SSD_HEADS]
    wq, wk, wv, wz, wx, wdt, wu = jnp.split(jnp.transpose(w_in), s, axis=0)
    wdt = jnp.pad(wdt, ((0, DT_PAD - SSD_HEADS), (0, 0)))
    w_t = jnp.concatenate([wq, wk, wv, wz, wx, wu, wdt], axis=0).astype(BF16)
    pad_h = lambda v: jnp.pad(v.astype(F32), (0, DT_PAD - SSD_HEADS)).reshape(1, DT_PAD)
    pw_bd = jnp.zeros((POOL_W, POOL_W), F32)
    for gi in range(len(POOL_WINDOWS)):
        sl = slice(gi * POOL_GROUP_W, (gi + 1) * POOL_GROUP_W)
        pw_bd = pw_bd.at[sl, sl].set(pool_w[gi].astype(F32))
    return dict(
        g1=norm1.reshape(1, D_MODEL), w_t=w_t,
        qg=jnp.tile(q_norm, N_HEADS).reshape(1, ATT_W), kg=jnp.tile(k_norm, N_HEADS).reshape(1, ATT_W),
        cw=conv_w, cb=conv_b.reshape(1, CONV_DIM), dtb=pad_h(dt_bias), alg=pad_h(a_log),
        dsk=jnp.repeat(d_skip.astype(F32), SSD_HEADDIM).reshape(1, SSD_W), sg=ssd_norm.reshape(1, SSD_W),
        pw_bd=pw_bd.astype(BF16), ps=pool_scale.reshape(1, POOL_W),
        wo=w_out.astype(BF16), g2=norm2.reshape(1, D_MODEL), wgu=w_gu.astype(BF16), wdn=w_down.astype(BF16))


def _head_mean_matrix():
    r = jnp.arange(ATT_W) // HEAD_DIM
    return jnp.where(r[:, None] == r[None, :], 1.0 / HEAD_DIM, 0.0).astype(BF16)


def _layer(x, p, e_mat, conv_prev, ssm_prev, pool_prev, attend, *, pos0, tm, is_prompt):
    n, t, _ = x.shape
    x2d = x.reshape(n * t, D_MODEL)
    pr = _inproj(x2d, p["g1"], p["w_t"], p["qg"], p["kg"], e_mat, tm=tm, seq_len=t if is_prompt else None)
    seq = lambda a: a.reshape(n, t, a.shape[-1])
    q, z, xbc, u, dt = [seq(pr[name]) for name in ("q", "z", "xbc", "u", "dt")]
    if is_prompt:
        kmean = pr["kmean"].reshape(n, t // MOBA_BLOCK, ATT_W)
        att = attend(q, seq(pr["k_aug"]), seq(pr["v_aug"]), kmean)
        k_out, v_out = [jnp.transpose(pr[name].reshape(n, N_HEADS, HEAD_DIM, t), (0, 3, 1, 2))
                        for name in ("kt", "vt")]
    else:
        att = attend(q, seq(pr["k"]), seq(pr["v"]), None)
        k_out, v_out = [pr[name].reshape(n, t, N_HEADS, HEAD_DIM) for name in ("k", "v")]

    tp = -(-t // SSD_CHUNK) * SSD_CHUNK
    pad_t = lambda a: a if tp == t else jnp.pad(a, ((0, 0), (0, tp - t), (0, 0)))
    conv_init = jnp.pad(conv_prev, ((0, 0), (_CONV_PAD - (CONV_W - 1), 0), (0, 0)))
    pool_init = jnp.pad(pool_prev, ((0, 0), (_POOL_PAD - POOL_HIST, 0), (0, 0)))
    y, pool, ssm_new = _mixer(pad_t(xbc), pad_t(dt), pad_t(z), pad_t(u), conv_init, pool_init, ssm_prev,
                              p["cw"], p["cb"], p["dtb"], p["alg"], p["dsk"], p["sg"], p["pw_bd"], p["ps"],
                              pos0=pos0, valid_len=t)
    y = y[:, :t].reshape(n * t, SSD_W)
    pool = pool[:, :t].reshape(n * t, POOL_W)

    out = _outffn(x2d, att.reshape(n * t, ATT_W), y, pool, p["wo"], p["g2"], p["wgu"], p["wdn"], tm=tm)
    conv_state = jnp.concatenate([conv_prev, xbc], axis=1)[:, -(CONV_W - 1):]
    pool_state = jnp.concatenate([pool_prev, u], axis=1)[:, -POOL_HIST:]
    return out.reshape(n, t, D_MODEL), k_out, v_out, conv_state, ssm_new, pool_state


def kernel(x_prompt, x_sample, cache_k, cache_v, page_table, state_ssm, state_conv, state_pool, norm1, w_in, q_norm, k_norm, conv_w, conv_b, dt_bias, a_log, d_skip, ssd_norm, pool_w, pool_scale, w_out, norm2, w_gu, w_down):
    depth = w_in.shape[0]
    b, s, _ = x_prompt.shape
    nd, ts, _ = x_sample.shape
    past_len = page_table.shape[1] * PAGE_SIZE
    e_mat = _head_mean_matrix()
    conv0 = jnp.zeros((b, CONV_W - 1, CONV_DIM), x_prompt.dtype)
    ssm0 = jnp.zeros((b, SSD_HEADS, SSD_HEADDIM, D_STATE), F32)
    pool0 = jnp.zeros((b, POOL_HIST, POOL_W), x_prompt.dtype)
    ck_t, cv_t = _cache_pages_t(cache_k), _cache_pages_t(cache_v)
    yp, ys = x_prompt, x_sample
    acc = [[] for _ in range(10)]
    for l in range(depth):
        p = _prep_layer_params(norm1[l], w_in[l], q_norm[l], k_norm[l], conv_w[l], conv_b[l], dt_bias[l],
                               a_log[l], d_skip[l], ssd_norm[l], pool_w[l], pool_scale[l], w_out[l],
                               norm2[l], w_gu[l], w_down[l])
        attend_p = lambda q, k, v, kmean: _prompt_attn(q, k, v, kmean)
        yp, kp, vp, cp, hp, pp = _layer(yp, p, e_mat, conv0, ssm0, pool0, attend_p,
                                        pos0=0, tm=512, is_prompt=True)
        attend_s = lambda q, k, v, kmean, l=l: _sample_attn(q, k, v, ck_t, cv_t, page_table, l)
        ys, kk, vv, cc, hh, pq = _layer(ys, p, e_mat, state_conv[l], state_ssm[l], state_pool[l], attend_s,
                                        pos0=past_len, tm=nd * ts, is_prompt=False)
        for lst, val in zip(acc, (kp, vp, kk, vv, hp, hh, cp, cc, pp, pq)):
            lst.append(val)
    return (yp, ys) + tuple(jnp.stack(a) for a in acc)
```

```python
import functools
import math

import jax
import jax.numpy as jnp
from jax import lax
from jax.experimental import pallas as pl
from jax.experimental.pallas import tpu as pltpu

D_MODEL = 1024
ATT_W = 512
SSD_W = 256
POOL_W = 256
HEAD_DIM = 64
N_HEADS = ATT_W // HEAD_DIM
MOBA_BLOCK = 256
MOBA_TOPK = 3
ATT_SCALE = HEAD_DIM ** -0.5
SSD_HEADDIM = 64
SSD_HEADS = SSD_W // SSD_HEADDIM
SSD_GROUPS = 2
HEADS_PER_GROUP = SSD_HEADS // SSD_GROUPS
D_STATE = 128
SSD_BC = SSD_GROUPS * D_STATE
CONV_W = 4
CONV_DIM = SSD_W + 2 * SSD_BC
SSD_CHUNK = 128
POOL_WINDOWS = (2, 4, 8, 16)
POOL_GROUP_W = POOL_W // len(POOL_WINDOWS)
POOL_HIST = max(POOL_WINDOWS) - 1
PAGE_SIZE = 128
RMS_EPS = 1e-6

LANES = 128
SUBLANES = 8
VMEM_LIMIT_BYTES = 56 * 1024 * 1024

NEG_BIG = -1e30
LOG2E = math.log2(math.e)
DT_PAD = LANES
IN_W_PAD = 3 * ATT_W + SSD_W + CONV_DIM + POOL_W + DT_PAD

F32 = jnp.float32
BF16 = jnp.bfloat16
HIGHEST = lax.Precision.HIGHEST

_NT = (((1,), (1,)), ((), ()))
_TN = (((0,), (0,)), ((), ()))


def _silu(x):
    return x * (0.5 * jnp.tanh(0.5 * x) + 0.5)


def _softplus(x):
    return jnp.maximum(x, 0.0) + jnp.log1p(jnp.exp(-jnp.abs(x)))


def _cparams(*sem, **kw):
    return pltpu.CompilerParams(dimension_semantics=sem, vmem_limit_bytes=VMEM_LIMIT_BYTES, **kw)


def _const_spec(shape):
    zeros = (0,) * len(shape)
    return pl.BlockSpec(shape, lambda *_: zeros, pipeline_mode=pl.Buffered(1))


def _inproj_kernel(x_ref, g1_ref, wt_ref, qg_ref, kg_ref, e_ref,
                   q_ref, z_ref, xbc_ref, u_ref, dt_ref, *kv_refs, n_blk):
    x = x_ref[...]
    xn = x * lax.rsqrt(jnp.mean(x * x, axis=-1, keepdims=True) + RMS_EPS) * g1_ref[...]
    y = lax.dot_general(xn.astype(BF16), wt_ref[...], _NT, preferred_element_type=F32)

    def head_norm(t, g):
        ms = jnp.dot((t * t).astype(BF16), e_ref[...], preferred_element_type=F32)
        return t * lax.rsqrt(ms + RMS_EPS) * g

    o = 0
    q_ref[...] = head_norm(y[:, o:o + ATT_W], qg_ref[...]); o += ATT_W
    kn = head_norm(y[:, o:o + ATT_W], kg_ref[...]); o += ATT_W
    vv = y[:, o:o + ATT_W]; o += ATT_W
    z_ref[...] = y[:, o:o + SSD_W]; o += SSD_W
    xbc_ref[...] = y[:, o:o + CONV_DIM]; o += CONV_DIM
    u_ref[...] = y[:, o:o + POOL_W]; o += POOL_W
    dt_ref[...] = y[:, o:o + DT_PAD]
    if n_blk == 0:
        k_ref, v_ref = kv_refs
        k_ref[...] = kn
        v_ref[...] = vv
    else:
        kb_ref, vb_ref, kt_ref, vt_ref, mean_ref = kv_refs
        tm = kn.shape[0]
        lane = lax.broadcasted_iota(jnp.int32, (tm, LANES), 1)
        kl = (lax.broadcasted_iota(jnp.int32, (tm, LANES), 0) % MOBA_BLOCK).astype(F32)
        k_slabs, v_slabs = [], []
        for c in range(0, ATT_W, LANES):
            for half in (0, 1):
                own = (lane // HEAD_DIM) == half
                in_aux = (lane >= _AUX_K_START[half]) & (lane < _AUX_K_START[half] + _AUX_K_LANES)
                k_slabs.append(jnp.where(own, kn[:, c:c + LANES], jnp.where(in_aux, kl, 0.0)).astype(BF16))
                v_slabs.append(jnp.where(own, vv[:, c:c + LANES],
                                         (lane == _ONES_LANE[half]).astype(F32)).astype(BF16))
        kb_ref[...] = jnp.concatenate(k_slabs, axis=-1)
        vb_ref[...] = jnp.concatenate(v_slabs, axis=-1)
        kt_ref[...] = kn.T
        vt_ref[...] = vv.T
        for b in range(n_blk):
            blk = kn[b * MOBA_BLOCK:(b + 1) * MOBA_BLOCK]
            mean_ref[b] = jnp.sum(blk, axis=0, keepdims=True) * (1.0 / MOBA_BLOCK)


def _inproj(x2d, g1, w_t, qg, kg, e_mat, *, tm, seq_len=None):
    rows = x2d.shape[0]
    assert rows % tm == 0
    row_spec = lambda w: pl.BlockSpec((tm, w), lambda i: (i, 0))
    names = ["q", "z", "xbc", "u", "dt"]
    out_shape = [jax.ShapeDtypeStruct((rows, w), F32) for w in (ATT_W, SSD_W, CONV_DIM, POOL_W, DT_PAD)]
    out_specs = [row_spec(s.shape[1]) for s in out_shape]
    if seq_len is None:
        n_blk = 0
        names += ["k", "v"]
        out_shape += [jax.ShapeDtypeStruct((rows, ATT_W), F32)] * 2
        out_specs += [row_spec(ATT_W)] * 2
    else:
        assert tm % MOBA_BLOCK == 0 and seq_len % tm == 0 and rows % seq_len == 0
        n_blk = tm // MOBA_BLOCK
        tps = seq_len // tm
        names += ["k_aug", "v_aug", "kt", "vt", "kmean"]
        out_shape += [jax.ShapeDtypeStruct((rows, 2 * ATT_W), BF16)] * 2
        out_specs += [row_spec(2 * ATT_W)] * 2
        out_shape += [jax.ShapeDtypeStruct((rows // seq_len, ATT_W, seq_len), F32)] * 2
        out_specs += [pl.BlockSpec((None, ATT_W, tm), lambda i: (i // tps, 0, i % tps))] * 2
        out_shape.append(jax.ShapeDtypeStruct((rows // MOBA_BLOCK, 1, ATT_W), F32))
        out_specs.append(pl.BlockSpec((n_blk, 1, ATT_W), lambda i: (i, 0, 0)))
    outs = pl.pallas_call(
        functools.partial(_inproj_kernel, n_blk=n_blk),
        grid=(rows // tm,),
        in_specs=[row_spec(D_MODEL), _const_spec((1, D_MODEL)), _const_spec((IN_W_PAD, D_MODEL)),
                  _const_spec((1, ATT_W)), _const_spec((1, ATT_W)), _const_spec((ATT_W, ATT_W))],
        out_specs=out_specs,
        out_shape=out_shape,
        compiler_params=_cparams("arbitrary"),
        name="inproj",
    )(x2d, g1, w_t, qg, kg, e_mat)
    return dict(zip(names, outs))


_AUX_K_LANES = 3
_AUX_K_START = (HEAD_DIM, 0)
_ONES_LANE = (HEAD_DIM, HEAD_DIM - 1)
_ACC_ROWS = HEAD_DIM + SUBLANES
_ACC_ROW0 = (0, HEAD_DIM - SUBLANES)
_ATTN_Q_BLOCKS_PER_STEP = (4, 2, 1)


def _bf16_terms(x, n):
    terms = []
    for _ in range(n):
        t = x.astype(BF16).astype(F32)
        terms.append(t)
        x = x - t
    return terms


def _prompt_attn_kernel(q_ref, k_ref, v_ref, km_ref, o_ref, *scratch, nb, qb):
    per_set = len(scratch) // 2
    for t_ in range(qb):
        rows = pl.ds(t_ * MOBA_BLOCK, MOBA_BLOCK)
        scr = scratch[(t_ % 2) * per_set:(t_ % 2 + 1) * per_set]
        _prompt_attn_block(pl.program_id(2) * qb + t_, q_ref.at[rows], k_ref, v_ref, km_ref, o_ref.at[rows],
                           *scr, nb=nb)


def _prompt_attn_block(i, q_ref, k_ref, v_ref, km_ref, o_ref,
                       bias_sc, qs_sc, s_a, s_b, s_d, p_a, p_b, al_a, al_b, m_sc, acc_sc, *, nb):
    j = pl.program_id(1)
    blk = MOBA_BLOCK
    w2 = 2 * blk

    lane = lax.broadcasted_iota(jnp.int32, (1, w2), 1)
    head = 2 * j + (lane >= blk).astype(jnp.int32)
    slope2 = jnp.exp2(-(head + 1).astype(F32)) * LOG2E

    q2 = q_ref[...]
    lane_q = lax.broadcasted_iota(jnp.int32, q2.shape, 1)
    for half in (0, 1):
        own = (lane_q // HEAD_DIM) == half
        aux = jnp.zeros(q2.shape, F32)
        slope_h = jnp.exp2(-(2 * j + half + 1).astype(F32)) * LOG2E
        for n_, term in enumerate(_bf16_terms(jnp.full((1, 1), slope_h, F32), _AUX_K_LANES)):
            aux = jnp.where(lane_q == _AUX_K_START[half] + n_, term, aux)
        qs_sc[half] = jnp.where(own, q2 * (ATT_SCALE * LOG2E), aux).astype(BF16)

    def qk(b, dst):
        start = pl.multiple_of(b * blk, blk)
        for half in (0, 1):
            dst[:, half * blk:(half + 1) * blk] = lax.dot_general(
                k_ref[pl.ds(start, blk), half * LANES:(half + 1) * LANES], qs_sc[half], _NT,
                preferred_element_type=F32)

    def softmax(s, r, p_ref, al_ref):
        m_prev = m_sc[...]
        cmax = jnp.max(s, axis=0, keepdims=True)
        m_new = jnp.maximum(m_prev, cmax + r)
        p_ref[...] = jnp.exp2(s - jnp.maximum(m_new - r, cmax)).astype(BF16)
        al_ref[...] = jnp.exp2(m_prev - m_new)
        m_sc[...] = m_new

    qm = jnp.concatenate([jnp.where(lane_q < HEAD_DIM, q2, 0.0),
                          jnp.where(lane_q >= HEAD_DIM, q2, 0.0)], axis=0)
    km_hi, km_lo = [t.astype(BF16) for t in _bf16_terms(km_ref[...], 2)]
    qm_hi, qm_lo = [t.astype(BF16) for t in _bf16_terms(qm, 2)]
    ntdot = lambda a, b: lax.dot_general(a, b, _NT, preferred_element_type=F32)
    gate = ntdot(km_hi, qm_hi) + (ntdot(km_hi, qm_lo) + ntdot(km_lo, qm_hi))

    qk(i, s_d)
    qk(0, s_a)
    m_sc[...] = jnp.full(m_sc.shape, NEG_BIG, F32)
    kl = lax.broadcasted_iota(jnp.int32, (blk, w2), 0)
    ql = lax.broadcasted_iota(jnp.int32, (blk, w2), 1) % blk
    softmax(jnp.where(kl <= ql, s_d[...], NEG_BIG), 0.0, p_b, al_b)

    bidx = lax.broadcasted_iota(jnp.int32, (nb, w2), 0)
    past = bidx < i
    gate = jnp.where(past, gate, -jnp.inf)
    rank = jnp.zeros((nb, w2), jnp.int32)
    for b in range(nb):
        gb = gate[b:b + 1, :]
        rank = rank + ((gb > gate) | ((gb == gate) & (b < bidx))).astype(jnp.int32)
    sel = past & (rank < MOBA_TOPK)
    bias_sc[...] = jnp.where(sel, -slope2 * ((i - bidx) * blk).astype(F32), NEG_BIG)

    def pv_acc(b, p_ref, al_ref):
        start = pl.multiple_of(b * blk, blk)
        for half in (0, 1):
            cols = slice(half * blk, (half + 1) * blk)
            pv = lax.dot_general(v_ref[pl.ds(start, blk), half * LANES:(half + 1) * LANES], p_ref[:, cols], _TN,
                                 preferred_element_type=F32)
            r0 = _ACC_ROW0[half]
            acc_sc[half] = al_ref[:, cols] * acc_sc[half] + pv[r0:r0 + _ACC_ROWS]

    def softmax_past(b, s_ref, p_ref, al_ref):
        softmax(s_ref[...], bias_sc[pl.ds(b, 1), :], p_ref, al_ref)

    acc_sc[...] = jnp.zeros(acc_sc.shape, F32)

    def body(p, carry):
        a = 2 * p
        softmax_past(a, s_a, p_a, al_a)
        qk(a + 1, s_b)
        pv_acc(jnp.where(p == 0, i, a - 1), p_b, al_b)
        softmax_past(a + 1, s_b, p_b, al_b)
        qk(jnp.minimum(a + 2, nb - 1), s_a)
        pv_acc(a, p_a, al_a)
        return carry

    n_pairs = (i + 1) // 2
    lax.fori_loop(0, n_pairs, body, 0)
    pv_acc(jnp.where(n_pairs == 0, i, 2 * n_pairs - 1), p_b, al_b)

    lo, hi = acc_sc[0], acc_sc[1]
    o2t = jnp.concatenate([lo[:HEAD_DIM] / lo[HEAD_DIM:HEAD_DIM + 1],
                           hi[SUBLANES:] / hi[SUBLANES - 1:SUBLANES]], axis=0)
    o_ref[...] = o2t.T


def _prompt_attn(q, k_aug, v_aug, kmean):
    n, t, _ = q.shape
    nb = t // MOBA_BLOCK
    assert t % MOBA_BLOCK == 0 and nb % 2 == 0 and kmean.shape == (n, nb, ATT_W)
    pairs = ATT_W // LANES
    blk = MOBA_BLOCK
    w2 = 2 * blk
    qb = max(d for d in _ATTN_Q_BLOCKS_PER_STEP if nb % d == 0)
    slab = pl.BlockSpec((None, t, 2 * LANES), lambda a, j, i: (a, 0, j))
    scratch_set = ([pltpu.VMEM((nb, w2), F32), pltpu.VMEM((2, blk, LANES), BF16)]
                   + [pltpu.VMEM((blk, w2), F32)] * 3 + [pltpu.VMEM((blk, w2), BF16)] * 2
                   + [pltpu.VMEM((1, w2), F32)] * 3 + [pltpu.VMEM((2, _ACC_ROWS, blk), F32)])
    return pl.pallas_call(
        functools.partial(_prompt_attn_kernel, nb=nb, qb=qb),
        grid=(n, pairs, nb // qb),
        in_specs=[pl.BlockSpec((None, qb * blk, LANES), lambda a, j, i: (a, i, j)), slab, slab,
                  pl.BlockSpec((None, nb, LANES), lambda a, j, i: (a, 0, j))],
        out_specs=pl.BlockSpec((None, qb * blk, LANES), lambda a, j, i: (a, i, j)),
        out_shape=jax.ShapeDtypeStruct((n, t, ATT_W), F32),
        scratch_shapes=scratch_set * 2,
        compiler_params=_cparams("arbitrary", "arbitrary", "arbitrary"),
        name="moba_prompt",
    )(q, k_aug, v_aug, kmean)


def _sample_attn_kernel(pt_ref, q_ref, kn_ref, vn_ref, *refs, t, ppc, n_blocks):
    del pt_ref
    kp = refs[:ppc]
    vp = refs[ppc:2 * ppc]
    o_ref = refs[2 * ppc]
    qm_sc, g_sc, m_sc, l_sc, o_sc = refs[2 * ppc + 1:]
    c = pl.program_id(1)
    rows = N_HEADS * t
    bpc = ppc * PAGE_SIZE // MOBA_BLOCK
    ppb = MOBA_BLOCK // PAGE_SIZE
    past_len = n_blocks * MOBA_BLOCK

    rid = lax.broadcasted_iota(jnp.int32, (rows, 1), 0)
    slope = jnp.exp2(-(rid // t + 1).astype(F32))
    posq = (past_len + rid % t).astype(F32)

    @pl.when(c == 0)
    def _():
        qt = q_ref[...]
        r2 = lax.broadcasted_iota(jnp.int32, qt.shape, 0)
        l2 = lax.broadcasted_iota(jnp.int32, qt.shape, 1)
        qm_sc[...] = jnp.where(r2 // t == l2 // HEAD_DIM, qt, 0.0)
        g_sc[...] = jnp.zeros(g_sc.shape, F32)
        m_sc[...] = jnp.full(m_sc.shape, NEG_BIG, F32)
        l_sc[...] = jnp.zeros(l_sc.shape, F32)

    qs = (qm_sc[...] * ATT_SCALE).astype(BF16)
    lane_b = lax.broadcasted_iota(jnp.int32, (rows, LANES), 1)
    kl = lax.broadcasted_iota(jnp.int32, (rows, MOBA_BLOCK), 1).astype(F32)

    scores = [jnp.dot(qs, kp[r][...].astype(BF16), preferred_element_type=F32) for r in range(ppc)]
    g_new, m_new, l_new = g_sc[...], m_sc[...], l_sc[...]
    probs = []
    for bl in range(bpc):
        gb = c * bpc + bl
        s = jnp.concatenate(scores[bl * ppb:(bl + 1) * ppb], axis=-1)
        gsum = jnp.sum(s, axis=-1, keepdims=True)
        s = s - slope * (posq - (gb * MOBA_BLOCK).astype(F32) - kl)
        mb = jnp.max(s, axis=-1, keepdims=True)
        p = jnp.exp(s - mb)
        lb = jnp.sum(p, axis=-1, keepdims=True)
        probs.append(p.astype(BF16))
        g_new = jnp.where(lane_b == gb, gsum, g_new)
        m_new = jnp.where(lane_b == gb, mb, m_new)
        l_new = jnp.where(lane_b == gb, lb, l_new)
    g_sc[...] = g_new
    m_sc[...] = m_new
    l_sc[...] = l_new
    for bl in range(bpc):
        ob = None
        for n_ in range(ppb):
            pr = probs[bl][:, n_ * PAGE_SIZE:(n_ + 1) * PAGE_SIZE]
            part = lax.dot_general(pr, vp[bl * ppb + n_][...].astype(BF16), _NT, preferred_element_type=F32)
            ob = part if ob is None else ob + part
        o_sc[c * bpc + bl] = ob

    @pl.when(c == pl.num_programs(1) - 1)
    def _():
        valid = lane_b < n_blocks
        gate = jnp.where(valid, g_sc[...], -jnp.inf)
        rank = jnp.zeros((rows, LANES), jnp.int32)
        for b in range(n_blocks):
            gcol = gate[:, b:b + 1]
            rank = rank + ((gcol > gate) | ((gcol == gate) & (b < lane_b))).astype(jnp.int32)
        sel = valid & (rank < MOBA_TOPK)
        kn = kn_ref[...]
        vn = vn_ref[...]
        tp = kn.shape[0]
        s_own = lax.dot_general(qs, kn.astype(BF16), _NT, preferred_element_type=F32)
        tk = lax.broadcasted_iota(jnp.int32, (rows, tp), 1)
        tq = lax.broadcasted_iota(jnp.int32, (rows, tp), 0) % t
        s_own = s_own - slope * (tq - tk).astype(F32)
        s_own = jnp.where(tk <= tq, s_own, NEG_BIG)
        m_blk = jnp.where(sel, m_sc[...], NEG_BIG)
        m = jnp.maximum(jnp.max(m_blk, axis=-1, keepdims=True), jnp.max(s_own, axis=-1, keepdims=True))
        w = jnp.where(sel, jnp.exp(m_blk - m), 0.0)
        p_own = jnp.exp(s_own - m)
        den = jnp.sum(w * l_sc[...], axis=-1, keepdims=True) + jnp.sum(p_own, axis=-1, keepdims=True)
        num = jnp.dot(p_own.astype(BF16), vn.astype(BF16), preferred_element_type=F32)
        for b in range(n_blocks):
            num = num + w[:, b:b + 1] * o_sc[b]
        out = num / den
        r2 = lax.broadcasted_iota(jnp.int32, out.shape, 0)
        l2 = lax.broadcasted_iota(jnp.int32, out.shape, 1)
        out = jnp.where(r2 // t == l2 // HEAD_DIM, out, 0.0)
        pick = (lax.broadcasted_iota(jnp.int32, (tp, rows), 1) % t
                == lax.broadcasted_iota(jnp.int32, (tp, rows), 0)).astype(F32)
        o_ref[...] = jnp.dot(pick, out, precision=HIGHEST, preferred_element_type=F32)


def _cache_pages_t(cache):
    d, n_pool = cache.shape[:2]
    return jnp.transpose(cache, (0, 1, 3, 4, 2)).reshape(d, n_pool, ATT_W, PAGE_SIZE)


def _sample_attn(q, k_new, v_new, ck_t, cv_t, page_table, layer, *, ppc=16):
    n, t, _ = q.shape
    n_pages = page_table.shape[1]
    assert (n_pages * PAGE_SIZE) % MOBA_BLOCK == 0, "cached length must be whole MoBA blocks"
    assert n_pages % ppc == 0 and (ppc * PAGE_SIZE) % MOBA_BLOCK == 0
    n_blocks = n_pages * PAGE_SIZE // MOBA_BLOCK
    assert n_blocks <= LANES
    rows = N_HEADS * t
    tp = -(-t // SUBLANES) * SUBLANES
    q_rep = jnp.tile(q, (1, N_HEADS, 1))
    pad_t = lambda a: jnp.pad(a, ((0, 0), (0, tp - t), (0, 0)))

    def page_spec(r):
        return pl.BlockSpec((None, None, ATT_W, PAGE_SIZE),
                            lambda a, c, pt: (layer, pt[a * n_pages + c * ppc + r], 0, 0))

    tok_spec = lambda r: pl.BlockSpec((None, r, ATT_W), lambda a, c, pt: (a, 0, 0))
    grid_spec = pltpu.PrefetchScalarGridSpec(
        num_scalar_prefetch=1,
        grid=(n, n_pages // ppc),
        in_specs=[tok_spec(rows), tok_spec(tp), tok_spec(tp)] + [page_spec(r) for r in range(ppc)] * 2,
        out_specs=tok_spec(tp),
        scratch_shapes=[pltpu.VMEM((rows, ATT_W), F32), pltpu.VMEM((rows, LANES), F32),
                        pltpu.VMEM((rows, LANES), F32), pltpu.VMEM((rows, LANES), F32),
                        pltpu.VMEM((n_blocks, rows, ATT_W), F32)],
    )
    out = pl.pallas_call(
        functools.partial(_sample_attn_kernel, t=t, ppc=ppc, n_blocks=n_blocks),
        grid_spec=grid_spec,
        out_shape=jax.ShapeDtypeStruct((n, tp, ATT_W), F32),
        compiler_params=_cparams("arbitrary", "arbitrary"),
        name="moba_sample",
    )(page_table.reshape(-1), q_rep, pad_t(k_new), pad_t(v_new), *([ck_t] * ppc), *([cv_t] * ppc))
    return out[:, :t]


_CONV_PAD = SUBLANES
_POOL_PAD = 2 * SUBLANES


def _mixer_kernel(xbc_ref, dt_ref, z_ref, u_ref, cinit_ref, pinit_ref, sinit_ref,
                  cw_ref, cb_ref, dtb_ref, alg_ref, dsk_ref, sg_ref, pw_ref, ps_ref,
                  y_ref, po_ref, sfin_ref,
                  cext, pext, a1, a2, a3, st_sc, *stage, L, pos0, valid_len):
    c = pl.program_id(1)
    short = valid_len < L

    def chunk_rows(ref, stage_ref):
        if not short:
            return ref[...]
        stage_ref[...] = jnp.zeros(stage_ref.shape, F32)
        stage_ref[0:valid_len] = ref[...]
        return stage_ref[...]

    @pl.when(c == 0)
    def _():
        cext[0:_CONV_PAD] = cinit_ref[...]
        pext[0:_POOL_PAD] = pinit_ref[...]
        st_sc[...] = sinit_ref[...].reshape(SSD_W, D_STATE)

    row = lax.broadcasted_iota(jnp.int32, (L, 1), 0)

    if short:
        cext[_CONV_PAD:_CONV_PAD + L] = jnp.zeros((L, CONV_DIM), F32)
        cext[_CONV_PAD:_CONV_PAD + valid_len] = xbc_ref[...]
    else:
        cext[_CONV_PAD:_CONV_PAD + L] = xbc_ref[...]
    conv = cb_ref[...] + cext[_CONV_PAD - 3:_CONV_PAD - 3 + L] * cw_ref[0:1, :]
    for i in range(1, CONV_W):
        conv = conv + cext[_CONV_PAD - 3 + i:_CONV_PAD - 3 + i + L] * cw_ref[i:i + 1, :]
    cext[0:_CONV_PAD] = cext[L:L + _CONV_PAD]
    conv = _silu(conv)
    xs = conv[:, 0:SSD_W]
    bm = conv[:, SSD_W:SSD_W + SSD_BC]
    cm = conv[:, SSD_W + SSD_BC:]

    dt = _softplus(chunk_rows(dt_ref, stage[0] if short else None) + dtb_ref[...])
    if short:
        dt = jnp.where(row < valid_len, dt, 0.0)
    a = -jnp.exp(alg_ref[...])
    da = dt * a
    ri = lax.broadcasted_iota(jnp.int32, (L, L), 0)
    ci = lax.broadcasted_iota(jnp.int32, (L, L), 1)
    causal = ci <= ri
    tril = causal.astype(F32)
    acum = jnp.dot(tril, da, precision=HIGHEST, preferred_element_type=F32)
    acum_t = acum.T
    a_last = acum[L - 1:L, :]

    lane_head = lax.broadcasted_iota(jnp.int32, (1, SSD_W), 1) // SSD_HEADDIM
    dt_b = jnp.zeros((L, SSD_W), F32)
    for h in range(SSD_HEADS):
        dt_b = jnp.where(lane_head == h, dt[:, h:h + 1], dt_b)
    xd = xs * dt_b
    xd_t = xd.T

    state = st_sc[...]
    y = jnp.zeros((L, SSD_W), F32)
    new_state = []
    for g in range(SSD_GROUPS):
        bg = bm[:, g * D_STATE:(g + 1) * D_STATE]
        cg = cm[:, g * D_STATE:(g + 1) * D_STATE]
        cb = lax.dot_general(cg, bg, _NT, preferred_element_type=F32)
        for hh in range(HEADS_PER_GROUP):
            h = g * HEADS_PER_GROUP + hh
            col = acum[:, h:h + 1]
            diff = col - acum_t[h:h + 1, :]
            lmat = jnp.exp(jnp.where(causal, diff, NEG_BIG))
            y_diag = jnp.dot(cb * lmat, xd, preferred_element_type=F32)
            y_off = lax.dot_general(cg * jnp.exp(col), state, _NT, preferred_element_type=F32)
            y = jnp.where(lane_head == h, y_diag + y_off, y)
            decay = jnp.exp(a_last[:, h:h + 1] - col)
            st = jnp.dot(xd_t, bg * decay, preferred_element_type=F32)
            lo = h * SSD_HEADDIM
            new_state.append(jnp.exp(a_last[:, h:h + 1]) * state[lo:lo + SSD_HEADDIM]
                             + st[lo:lo + SSD_HEADDIM])
    state = jnp.concatenate(new_state, axis=0)
    st_sc[...] = state

    y = y + dsk_ref[...] * xs
    yz = y * _silu(chunk_rows(z_ref, stage[1] if short else None))
    y_out = yz * lax.rsqrt(jnp.mean(yz * yz, axis=-1, keepdims=True) + RMS_EPS) * sg_ref[...]
    y_ref[...] = y_out[:valid_len] if short else y_out

    @pl.when(c == pl.num_programs(1) - 1)
    def _():
        sfin_ref[...] = state.reshape(SSD_HEADS, SSD_HEADDIM, D_STATE)

    P = _POOL_PAD
    if short:
        pext[P:P + L] = jnp.zeros((L, POOL_W), F32)
        pext[P:P + valid_len] = u_ref[...]
    else:
        pext[P:P + L] = u_ref[...]
    u = pext[P:P + L]
    n_ext = P + L
    a1[1:n_ext] = pext[1:n_ext] + pext[0:n_ext - 1]
    a2[3:n_ext] = a1[3:n_ext] + a1[1:n_ext - 2]
    a3[7:n_ext] = a2[7:n_ext] + a2[3:n_ext - 4]
    s16 = a3[P:n_ext] + a3[P - 8:n_ext - 8]
    pext[0:P] = pext[L:L + P]
    pos1 = pos0 + c * L + row + 1
    lane_grp = lax.broadcasted_iota(jnp.int32, (1, POOL_W), 1) // POOL_GROUP_W
    sums = (a1[P:n_ext], a2[P:n_ext], a3[P:n_ext], s16)
    win = s16
    width = jnp.full((1, POOL_W), POOL_WINDOWS[-1], jnp.int32)
    for gi, w in enumerate(POOL_WINDOWS[:-1]):
        win = jnp.where(lane_grp == gi, sums[gi], win)
        width = jnp.where(lane_grp == gi, w, width)
    cnt = jnp.minimum(width, pos1).astype(F32)
    d = win * (1.0 / cnt) - u
    po = jnp.dot(d.astype(BF16), pw_ref[...], preferred_element_type=F32) * ps_ref[...]
    po_ref[...] = po[:valid_len] if short else po


def _mixer(xbc, dt, z, u, conv_init, pool_init, ssm_init, cw, cb, dtb, alg, dsk, sg, pw_bd, ps,
           *, pos0):
    n, t, _ = xbc.shape
    L = SSD_CHUNK
    assert t % L == 0 or t < L, "whole chunks, or one short chunk"
    valid_len = min(t, L)
    seq = lambda w: pl.BlockSpec((None, valid_len, w), lambda a, c: (a, c, 0))
    per_seq = lambda *s: pl.BlockSpec((None,) + s, lambda a, c: (a,) + (0,) * len(s))
    return pl.pallas_call(
        functools.partial(_mixer_kernel, L=L, pos0=pos0, valid_len=valid_len),
        grid=(n, -(-t // L)),
        in_specs=[seq(CONV_DIM), seq(DT_PAD), seq(SSD_W), seq(POOL_W),
                  per_seq(_CONV_PAD, CONV_DIM), per_seq(_POOL_PAD, POOL_W),
                  per_seq(SSD_HEADS, SSD_HEADDIM, D_STATE),
                  _const_spec((CONV_W, CONV_DIM)), _const_spec((1, CONV_DIM)), _const_spec((1, DT_PAD)),
                  _const_spec((1, DT_PAD)), _const_spec((1, SSD_W)), _const_spec((1, SSD_W)),
                  _const_spec((POOL_W, POOL_W)), _const_spec((1, POOL_W))],
        out_specs=[seq(SSD_W), seq(POOL_W), per_seq(SSD_HEADS, SSD_HEADDIM, D_STATE)],
        out_shape=[jax.ShapeDtypeStruct((n, t, SSD_W), F32), jax.ShapeDtypeStruct((n, t, POOL_W), F32),
                   jax.ShapeDtypeStruct((n, SSD_HEADS, SSD_HEADDIM, D_STATE), F32)],
        scratch_shapes=[pltpu.VMEM((_CONV_PAD + L, CONV_DIM), F32), pltpu.VMEM((_POOL_PAD + L, POOL_W), F32),
                        pltpu.VMEM((_POOL_PAD + L, POOL_W), F32), pltpu.VMEM((_POOL_PAD + L, POOL_W), F32),
                        pltpu.VMEM((_POOL_PAD + L, POOL_W), F32), pltpu.VMEM((SSD_W, D_STATE), F32)]
                       + ([pltpu.VMEM((L, DT_PAD), F32), pltpu.VMEM((L, SSD_W), F32)] if valid_len < L else []),
        compiler_params=_cparams("arbitrary", "arbitrary"),
        name="mixer",
    )(xbc, dt, z, u, conv_init, pool_init, ssm_init, cw, cb, dtb, alg, dsk, sg, pw_bd, ps)


_FF_CHUNK = 256


def _outffn_kernel(x_ref, att_ref, y_ref, p_ref, wo_ref, g2_ref, wgu_ref, wdn_ref, o_ref, *, d_ff):
    mix = jnp.concatenate([att_ref[...], y_ref[...], p_ref[...]], axis=-1).astype(BF16)
    h = x_ref[...] + jnp.dot(mix, wo_ref[...], preferred_element_type=F32)
    hn = (h * lax.rsqrt(jnp.mean(h * h, axis=-1, keepdims=True) + RMS_EPS) * g2_ref[...]).astype(BF16)
    acc = h
    for c0 in range(0, d_ff, _FF_CHUNK):
        gate = jnp.dot(hn, wgu_ref[:, c0:c0 + _FF_CHUNK], preferred_element_type=F32)
        up = jnp.dot(hn, wgu_ref[:, d_ff + c0:d_ff + c0 + _FF_CHUNK], preferred_element_type=F32)
        ff = (_silu(gate) * up).astype(BF16)
        acc = acc + jnp.dot(ff, wdn_ref[c0:c0 + _FF_CHUNK, :], preferred_element_type=F32)
    o_ref[...] = acc


def _outffn(x2d, att, y, pool, wo, g2, wgu, wdn, *, tm):
    rows = x2d.shape[0]
    d_ff = wdn.shape[0]
    assert rows % tm == 0 and d_ff % _FF_CHUNK == 0
    row_spec = lambda w: pl.BlockSpec((tm, w), lambda i: (i, 0))
    return pl.pallas_call(
        functools.partial(_outffn_kernel, d_ff=d_ff),
        grid=(rows // tm,),
        in_specs=[row_spec(D_MODEL), row_spec(ATT_W), row_spec(SSD_W), row_spec(POOL_W),
                  _const_spec((D_MODEL, D_MODEL)), _const_spec((1, D_MODEL)),
                  _const_spec((D_MODEL, 2 * d_ff)), _const_spec((d_ff, D_MODEL))],
        out_specs=row_spec(D_MODEL),
        out_shape=jax.ShapeDtypeStruct((rows, D_MODEL), F32),
        compiler_params=_cparams("arbitrary"),
        name="outffn",
    )(x2d, att, y, pool, wo, g2, wgu, wdn)


def _prep_layer_params(norm1, w_in, q_norm, k_norm, conv_w, conv_b, dt_bias, a_log, d_skip, ssd_norm,
                       pool_w, pool_scale, w_out, norm2, w_gu, w_down):
    s = [ATT_W, 2 * ATT_W, 3 * ATT_W, 3 * ATT_W + SSD_W, 3 * ATT_W + SSD_W + CONV_DIM,
         3 * ATT_W + SSD_W + CONV_DIM + SSD_HEADS]
    wq, wk, wv, wz, wx, wdt, wu = jnp.split(jnp.transpose(w_in), s, axis=0)
    wdt = jnp.pad(wdt, ((0, DT_PAD - SSD_HEADS), (0, 0)))
    w_t = jnp.concatenate([wq, wk, wv, wz, wx, wu, wdt], axis=0).astype(BF16)
    pad_h = lambda v: jnp.pad(v.astype(F32), (0, DT_PAD - SSD_HEADS)).reshape(1, DT_PAD)
    pw_bd = jnp.zeros((POOL_W, POOL_W), F32)
    for gi in range(len(POOL_WINDOWS)):
        sl = slice(gi * POOL_GROUP_W, (gi + 1) * POOL_GROUP_W)
        pw_bd = pw_bd.at[sl, sl].set(pool_w[gi].astype(F32))
    return dict(
        g1=norm1.reshape(1, D_MODEL), w_t=w_t,
        qg=jnp.tile(q_norm, N_HEADS).reshape(1, ATT_W), kg=jnp.tile(k_norm, N_HEADS).reshape(1, ATT_W),
        cw=conv_w, cb=conv_b.reshape(1, CONV_DIM), dtb=pad_h(dt_bias), alg=pad_h(a_log),
        dsk=jnp.repeat(d_skip.astype(F32), SSD_HEADDIM).reshape(1, SSD_W), sg=ssd_norm.reshape(1, SSD_W),
        pw_bd=pw_bd.astype(BF16), ps=pool_scale.reshape(1, POOL_W),
        wo=w_out.astype(BF16), g2=norm2.reshape(1, D_MODEL), wgu=w_gu.astype(BF16), wdn=w_down.astype(BF16))


def _head_mean_matrix():
    r = jnp.arange(ATT_W) // HEAD_DIM
    return jnp.where(r[:, None] == r[None, :], 1.0 / HEAD_DIM, 0.0).astype(BF16)


def _layer(x, p, e_mat, conv_prev, ssm_prev, pool_prev, attend, *, pos0, tm, is_prompt):
    n, t, _ = x.shape
    x2d = x.reshape(n * t, D_MODEL)
    pr = _inproj(x2d, p["g1"], p["w_t"], p["qg"], p["kg"], e_mat, tm=tm, seq_len=t if is_prompt else None)
    seq = lambda a: a.reshape(n, t, a.shape[-1])
    q, z, xbc, u, dt = [seq(pr[name]) for name in ("q", "z", "xbc", "u", "dt")]
    if is_prompt:
        kmean = pr["kmean"].reshape(n, t // MOBA_BLOCK, ATT_W)
        att = attend(q, seq(pr["k_aug"]), seq(pr["v_aug"]), kmean)
        k_out, v_out = [jnp.transpose(pr[name].reshape(n, N_HEADS, HEAD_DIM, t), (0, 3, 1, 2))
                        for name in ("kt", "vt")]
    else:
        att = attend(q, seq(pr["k"]), seq(pr["v"]), None)
        k_out, v_out = [pr[name].reshape(n, t, N_HEADS, HEAD_DIM) for name in ("k", "v")]

    conv_init = jnp.pad(conv_prev, ((0, 0), (_CONV_PAD - (CONV_W - 1), 0), (0, 0)))
    pool_init = jnp.pad(pool_prev, ((0, 0), (_POOL_PAD - POOL_HIST, 0), (0, 0)))
    y, pool, ssm_new = _mixer(xbc, dt, z, u, conv_init, pool_init, ssm_prev,
                              p["cw"], p["cb"], p["dtb"], p["alg"], p["dsk"], p["sg"], p["pw_bd"], p["ps"], pos0=pos0)
    y = y.reshape(n * t, SSD_W)
    pool = pool.reshape(n * t, POOL_W)

    out = _outffn(x2d, att.reshape(n * t, ATT_W), y, pool, p["wo"], p["g2"], p["wgu"], p["wdn"], tm=tm)
    conv_state = jnp.concatenate([conv_prev, xbc], axis=1)[:, -(CONV_W - 1):]
    pool_state = jnp.concatenate([pool_prev, u], axis=1)[:, -POOL_HIST:]
    return out.reshape(n, t, D_MODEL), k_out, v_out, conv_state, ssm_new, pool_state


def kernel(x_prompt, x_sample, cache_k, cache_v, page_table, state_ssm, state_conv, state_pool, norm1, w_in, q_norm, k_norm, conv_w, conv_b, dt_bias, a_log, d_skip, ssd_norm, pool_w, pool_scale, w_out, norm2, w_gu, w_down):
    depth = w_in.shape[0]
    b, s, _ = x_prompt.shape
    nd, ts, _ = x_sample.shape
    past_len = page_table.shape[1] * PAGE_SIZE
    e_mat = _head_mean_matrix()
    conv0 = jnp.zeros((b, CONV_W - 1, CONV_DIM), x_prompt.dtype)
    ssm0 = jnp.zeros((b, SSD_HEADS, SSD_HEADDIM, D_STATE), F32)
    pool0 = jnp.zeros((b, POOL_HIST, POOL_W), x_prompt.dtype)
    ck_t, cv_t = _cache_pages_t(cache_k), _cache_pages_t(cache_v)
    yp, ys = x_prompt, x_sample
    acc = [[] for _ in range(10)]
    for l in range(depth):
        p = _prep_layer_params(norm1[l], w_in[l], q_norm[l], k_norm[l], conv_w[l], conv_b[l], dt_bias[l],
                               a_log[l], d_skip[l], ssd_norm[l], pool_w[l], pool_scale[l], w_out[l],
                               norm2[l], w_gu[l], w_down[l])
        attend_p = lambda q, k, v, kmean: _prompt_attn(q, k, v, kmean)
        yp, kp, vp, cp, hp, pp = _layer(yp, p, e_mat, conv0, ssm0, pool0, attend_p,
                                        pos0=0, tm=512, is_prompt=True)
        attend_s = lambda q, k, v, kmean, l=l: _sample_attn(q, k, v, ck_t, cv_t, page_table, l)
        ys, kk, vv, cc, hh, pq = _layer(ys, p, e_mat, state_conv[l], state_ssm[l], state_pool[l], attend_s,
                                        pos0=past_len, tm=nd * ts, is_prompt=False)
        for lst, val in zip(acc, (kp, vp, kk, vv, hp, hh, cp, cc, pp, pq)):
            lst.append(val)
    return (yp, ys) + tuple(jnp.stack(a) for a in acc)
```

```python
import functools
import math

import jax
import jax.numpy as jnp
from jax import lax
from jax.experimental import pallas as pl
from jax.experimental.pallas import tpu as pltpu

D_MODEL = 1024
ATT_W = 512
SSD_W = 256
POOL_W = 256
HEAD_DIM = 64
N_HEADS = ATT_W // HEAD_DIM
MOBA_BLOCK = 256
MOBA_TOPK = 3
ATT_SCALE = HEAD_DIM ** -0.5
SSD_HEADDIM = 64
SSD_HEADS = SSD_W // SSD_HEADDIM
SSD_GROUPS = 2
HEADS_PER_GROUP = SSD_HEADS // SSD_GROUPS
D_STATE = 128
SSD_BC = SSD_GROUPS * D_STATE
CONV_W = 4
CONV_DIM = SSD_W + 2 * SSD_BC
SSD_CHUNK = 128
POOL_WINDOWS = (2, 4, 8, 16)
POOL_GROUP_W = POOL_W // len(POOL_WINDOWS)
POOL_HIST = max(POOL_WINDOWS) - 1
PAGE_SIZE = 128
RMS_EPS = 1e-6

LANES = 128
SUBLANES = 8
VMEM_LIMIT_BYTES = 56 * 1024 * 1024

NEG_BIG = -1e30
LOG2E = math.log2(math.e)
DT_PAD = LANES
IN_W_PAD = 3 * ATT_W + SSD_W + CONV_DIM + POOL_W + DT_PAD

F32 = jnp.float32
BF16 = jnp.bfloat16
HIGHEST = lax.Precision.HIGHEST

_NT = (((1,), (1,)), ((), ()))
_TN = (((0,), (0,)), ((), ()))


def _silu(x):
    return x * (0.5 * jnp.tanh(0.5 * x) + 0.5)


def _softplus(x):
    return jnp.maximum(x, 0.0) + jnp.log1p(jnp.exp(-jnp.abs(x)))


def _cparams(*sem, **kw):
    return pltpu.CompilerParams(dimension_semantics=sem, vmem_limit_bytes=VMEM_LIMIT_BYTES, **kw)


def _const_spec(shape):
    zeros = (0,) * len(shape)
    return pl.BlockSpec(shape, lambda *_: zeros, pipeline_mode=pl.Buffered(1))


def _inproj_kernel(x_ref, g1_ref, wt_ref, qg_ref, kg_ref, e_ref, *refs, n_blk):
    if n_blk:
        refs = refs[2:]
    q_ref, z_ref, xbc_ref, u_ref, dt_ref, *kv_refs = refs
    x = x_ref[...]
    xn = x * lax.rsqrt(jnp.mean(x * x, axis=-1, keepdims=True) + RMS_EPS) * g1_ref[...]
    y = lax.dot_general(xn.astype(BF16), wt_ref[...], _NT, preferred_element_type=F32)

    def head_norm(t, g):
        ms = jnp.dot((t * t).astype(BF16), e_ref[...], preferred_element_type=F32)
        return t * lax.rsqrt(ms + RMS_EPS) * g

    o = 0
    q_ref[...] = head_norm(y[:, o:o + ATT_W], qg_ref[...]); o += ATT_W
    kn = head_norm(y[:, o:o + ATT_W], kg_ref[...]); o += ATT_W
    vv = y[:, o:o + ATT_W]; o += ATT_W
    z_ref[...] = y[:, o:o + SSD_W]; o += SSD_W
    xbc_ref[...] = y[:, o:o + CONV_DIM]; o += CONV_DIM
    u_ref[...] = y[:, o:o + POOL_W]; o += POOL_W
    dt_ref[...] = y[:, o:o + DT_PAD]
    if n_blk == 0:
        k_ref, v_ref = kv_refs
        k_ref[...] = kn
        v_ref[...] = vv
    else:
        kb_ref, vb_ref, kt_ref, vt_ref, mean_ref = kv_refs
        tm = kn.shape[0]
        lane = lax.broadcasted_iota(jnp.int32, (tm, LANES), 1)
        kl = (lax.broadcasted_iota(jnp.int32, (tm, LANES), 0) % MOBA_BLOCK).astype(F32)
        k_slabs, v_slabs = [], []
        for c in range(0, ATT_W, LANES):
            for half in (0, 1):
                own = (lane // HEAD_DIM) == half
                in_aux = (lane >= _AUX_K_START[half]) & (lane < _AUX_K_START[half] + _AUX_K_LANES)
                k_slabs.append(jnp.where(own, kn[:, c:c + LANES], jnp.where(in_aux, kl, 0.0)).astype(BF16))
                v_slabs.append(jnp.where(own, vv[:, c:c + LANES],
                                         (lane == _ONES_LANE[half]).astype(F32)).astype(BF16))
        kb_ref[...] = jnp.concatenate(k_slabs, axis=-1)
        vb_ref[...] = jnp.concatenate(v_slabs, axis=-1)
        kt_ref[...] = kn.T
        vt_ref[...] = vv.T
        for b in range(n_blk):
            blk = kn[b * MOBA_BLOCK:(b + 1) * MOBA_BLOCK]
            mean_ref[b] = jnp.sum(blk, axis=0, keepdims=True) * (1.0 / MOBA_BLOCK)


def _inproj(x2d, g1, w_t, qg, kg, e_mat, *, tm, seq_len=None, kv_stack=None, layer=0):
    rows = x2d.shape[0]
    assert rows % tm == 0
    row_spec = lambda w: pl.BlockSpec((tm, w), lambda i: (i, 0))
    names = ["q", "z", "xbc", "u", "dt"]
    out_shape = [jax.ShapeDtypeStruct((rows, w), F32) for w in (ATT_W, SSD_W, CONV_DIM, POOL_W, DT_PAD)]
    out_specs = [row_spec(s.shape[1]) for s in out_shape]
    if seq_len is None:
        n_blk = 0
        names += ["k", "v"]
        out_shape += [jax.ShapeDtypeStruct((rows, ATT_W), F32)] * 2
        out_specs += [row_spec(ATT_W)] * 2
    else:
        assert tm % MOBA_BLOCK == 0 and seq_len % tm == 0 and rows % seq_len == 0
        n_blk = tm // MOBA_BLOCK
        tps = seq_len // tm
        names += ["k_aug", "v_aug", "kt", "vt", "kmean"]
        out_shape += [jax.ShapeDtypeStruct((rows, 2 * ATT_W), BF16)] * 2
        out_specs += [row_spec(2 * ATT_W)] * 2
        out_shape += [jax.ShapeDtypeStruct(a.shape, F32) for a in kv_stack]
        out_specs += [pl.BlockSpec((None, None, ATT_W, tm), lambda i: (layer, i // tps, 0, i % tps))] * 2
        out_shape.append(jax.ShapeDtypeStruct((rows // MOBA_BLOCK, 1, ATT_W), F32))
        out_specs.append(pl.BlockSpec((n_blk, 1, ATT_W), lambda i: (i, 0, 0)))
    in_specs = [row_spec(D_MODEL), _const_spec((1, D_MODEL)), _const_spec((IN_W_PAD, D_MODEL)),
                _const_spec((1, ATT_W)), _const_spec((1, ATT_W)), _const_spec((ATT_W, ATT_W))]
    args = [x2d, g1, w_t, qg, kg, e_mat]
    aliases = {}
    if seq_len is not None:
        aliases = {len(args) + n_: names.index(name) for n_, name in enumerate(("kt", "vt"))}
        in_specs += [pl.BlockSpec(memory_space=pl.ANY)] * 2
        args += list(kv_stack)
    outs = pl.pallas_call(
        functools.partial(_inproj_kernel, n_blk=n_blk),
        grid=(rows // tm,),
        in_specs=in_specs,
        out_specs=out_specs,
        out_shape=out_shape,
        input_output_aliases=aliases,
        compiler_params=_cparams("arbitrary"),
        name="inproj",
    )(*args)
    return dict(zip(names, outs))


_AUX_K_LANES = 3
_AUX_K_START = (HEAD_DIM, 0)
_ONES_LANE = (HEAD_DIM, HEAD_DIM - 1)
_ACC_ROWS = HEAD_DIM + SUBLANES
_ACC_ROW0 = (0, HEAD_DIM - SUBLANES)
_ATTN_Q_BLOCKS_PER_STEP = (8, 4, 2, 1)


def _bf16_terms(x, n):
    terms = []
    for _ in range(n):
        t = x.astype(BF16).astype(F32)
        terms.append(t)
        x = x - t
    return terms


def _prompt_attn_kernel(q_ref, k_ref, v_ref, km_ref, o_ref, *scratch, nb, qb):
    per_set = len(scratch) // 2
    for t_ in range(qb):
        rows = pl.ds(t_ * MOBA_BLOCK, MOBA_BLOCK)
        scr = scratch[(t_ % 2) * per_set:(t_ % 2 + 1) * per_set]
        _prompt_attn_block(pl.program_id(2) * qb + t_, q_ref.at[rows], k_ref, v_ref, km_ref, o_ref.at[rows],
                           *scr, nb=nb)


def _prompt_attn_block(i, q_ref, k_ref, v_ref, km_ref, o_ref,
                       bias_sc, qs_sc, s_a, s_b, s_d, p_a, p_b, al_a, al_b, m_sc, acc_sc, *, nb):
    j = pl.program_id(1)
    blk = MOBA_BLOCK
    w2 = 2 * blk

    lane = lax.broadcasted_iota(jnp.int32, (1, w2), 1)
    head = 2 * j + (lane >= blk).astype(jnp.int32)
    slope2 = jnp.exp2(-(head + 1).astype(F32)) * LOG2E

    q2 = q_ref[...]
    lane_q = lax.broadcasted_iota(jnp.int32, q2.shape, 1)
    for half in (0, 1):
        own = (lane_q // HEAD_DIM) == half
        aux = jnp.zeros(q2.shape, F32)
        slope_h = jnp.exp2(-(2 * j + half + 1).astype(F32)) * LOG2E
        for n_, term in enumerate(_bf16_terms(jnp.full((1, 1), slope_h, F32), _AUX_K_LANES)):
            aux = jnp.where(lane_q == _AUX_K_START[half] + n_, term, aux)
        qs_sc[half] = jnp.where(own, q2 * (ATT_SCALE * LOG2E), aux).astype(BF16)

    def qk(b, dst):
        start = pl.multiple_of(b * blk, blk)
        for half in (0, 1):
            dst[:, half * blk:(half + 1) * blk] = lax.dot_general(
                k_ref[pl.ds(start, blk), half * LANES:(half + 1) * LANES], qs_sc[half], _NT,
                preferred_element_type=F32)

    def softmax(s, r, p_ref, al_ref):
        m_prev = m_sc[...]
        cmax = jnp.max(s, axis=0, keepdims=True)
        m_new = jnp.maximum(m_prev, cmax + r)
        p_ref[...] = jnp.exp2(s - jnp.maximum(m_new - r, cmax)).astype(BF16)
        al_ref[...] = jnp.exp2(m_prev - m_new)
        m_sc[...] = m_new

    qm = jnp.concatenate([jnp.where(lane_q < HEAD_DIM, q2, 0.0),
                          jnp.where(lane_q >= HEAD_DIM, q2, 0.0)], axis=0)
    km_hi, km_lo = [t.astype(BF16) for t in _bf16_terms(km_ref[...], 2)]
    qm_hi, qm_lo = [t.astype(BF16) for t in _bf16_terms(qm, 2)]
    ntdot = lambda a, b: lax.dot_general(a, b, _NT, preferred_element_type=F32)
    gate = ntdot(km_hi, qm_hi) + (ntdot(km_hi, qm_lo) + ntdot(km_lo, qm_hi))

    qk(i, s_d)
    qk(0, s_a)
    m_sc[...] = jnp.full(m_sc.shape, NEG_BIG, F32)
    kl = lax.broadcasted_iota(jnp.int32, (blk, w2), 0)
    ql = lax.broadcasted_iota(jnp.int32, (blk, w2), 1) % blk
    softmax(jnp.where(kl <= ql, s_d[...], NEG_BIG), 0.0, p_b, al_b)

    bidx = lax.broadcasted_iota(jnp.int32, (nb, w2), 0)
    past = bidx < i
    gate = jnp.where(past, gate, -jnp.inf)
    rank = jnp.zeros((nb, w2), jnp.int32)
    for b in range(nb):
        gb = gate[b:b + 1, :]
        rank = rank + ((gb > gate) | ((gb == gate) & (b < bidx))).astype(jnp.int32)
    sel = past & (rank < MOBA_TOPK)
    bias_sc[...] = jnp.where(sel, -slope2 * ((i - bidx) * blk).astype(F32), NEG_BIG)

    def pv_acc(b, p_ref, al_ref):
        start = pl.multiple_of(b * blk, blk)
        for half in (0, 1):
            cols = slice(half * blk, (half + 1) * blk)
            pv = lax.dot_general(v_ref[pl.ds(start, blk), half * LANES:(half + 1) * LANES], p_ref[:, cols], _TN,
                                 preferred_element_type=F32)
            r0 = _ACC_ROW0[half]
            acc_sc[half] = al_ref[:, cols] * acc_sc[half] + pv[r0:r0 + _ACC_ROWS]

    def softmax_past(b, s_ref, p_ref, al_ref):
        softmax(s_ref[...], bias_sc[pl.ds(b, 1), :], p_ref, al_ref)

    acc_sc[...] = jnp.zeros(acc_sc.shape, F32)

    def body(p, carry):
        a = 2 * p
        softmax_past(a, s_a, p_a, al_a)
        qk(a + 1, s_b)
        pv_acc(jnp.where(p == 0, i, a - 1), p_b, al_b)
        softmax_past(a + 1, s_b, p_b, al_b)
        qk(jnp.minimum(a + 2, nb - 1), s_a)
        pv_acc(a, p_a, al_a)
        return carry

    n_pairs = (i + 1) // 2
    lax.fori_loop(0, n_pairs, body, 0)
    pv_acc(jnp.where(n_pairs == 0, i, 2 * n_pairs - 1), p_b, al_b)

    lo, hi = acc_sc[0], acc_sc[1]
    o2t = jnp.concatenate([lo[:HEAD_DIM] / lo[HEAD_DIM:HEAD_DIM + 1],
                           hi[SUBLANES:] / hi[SUBLANES - 1:SUBLANES]], axis=0)
    o_ref[...] = o2t.T


def _prompt_attn(q, k_aug, v_aug, kmean):
    n, t, _ = q.shape
    nb = t // MOBA_BLOCK
    assert t % MOBA_BLOCK == 0 and nb % 2 == 0 and kmean.shape == (n, nb, ATT_W)
    pairs = ATT_W // LANES
    blk = MOBA_BLOCK
    w2 = 2 * blk
    qb = max(d for d in _ATTN_Q_BLOCKS_PER_STEP if nb % d == 0)
    slab = pl.BlockSpec((None, t, 2 * LANES), lambda a, j, i: (a, 0, j))
    scratch_set = ([pltpu.VMEM((nb, w2), F32), pltpu.VMEM((2, blk, LANES), BF16)]
                   + [pltpu.VMEM((blk, w2), F32)] * 3 + [pltpu.VMEM((blk, w2), BF16)] * 2
                   + [pltpu.VMEM((1, w2), F32)] * 3 + [pltpu.VMEM((2, _ACC_ROWS, blk), F32)])
    return pl.pallas_call(
        functools.partial(_prompt_attn_kernel, nb=nb, qb=qb),
        grid=(n, pairs, nb // qb),
        in_specs=[pl.BlockSpec((None, qb * blk, LANES), lambda a, j, i: (a, i, j)), slab, slab,
                  pl.BlockSpec((None, nb, LANES), lambda a, j, i: (a, 0, j))],
        out_specs=pl.BlockSpec((None, qb * blk, LANES), lambda a, j, i: (a, i, j)),
        out_shape=jax.ShapeDtypeStruct((n, t, ATT_W), F32),
        scratch_shapes=scratch_set * 2,
        compiler_params=_cparams("arbitrary", "arbitrary", "arbitrary"),
        name="moba_prompt",
    )(q, k_aug, v_aug, kmean)


def _sample_attn_kernel(pt_ref, q_ref, kn_ref, vn_ref, *refs, t, ppc, n_blocks):
    del pt_ref
    kp = refs[:ppc]
    vp = refs[ppc:2 * ppc]
    o_ref = refs[2 * ppc]
    qm_sc, g_sc, m_sc, l_sc, o_sc = refs[2 * ppc + 1:]
    c = pl.program_id(1)
    rows = N_HEADS * t
    bpc = ppc * PAGE_SIZE // MOBA_BLOCK
    ppb = MOBA_BLOCK // PAGE_SIZE
    past_len = n_blocks * MOBA_BLOCK

    rid = lax.broadcasted_iota(jnp.int32, (rows, 1), 0)
    slope = jnp.exp2(-(rid // t + 1).astype(F32))
    posq = (past_len + rid % t).astype(F32)

    @pl.when(c == 0)
    def _():
        qt = q_ref[...]
        r2 = lax.broadcasted_iota(jnp.int32, qt.shape, 0)
        l2 = lax.broadcasted_iota(jnp.int32, qt.shape, 1)
        qm_sc[...] = jnp.where(r2 // t == l2 // HEAD_DIM, qt, 0.0)
        g_sc[...] = jnp.zeros(g_sc.shape, F32)
        m_sc[...] = jnp.full(m_sc.shape, NEG_BIG, F32)
        l_sc[...] = jnp.zeros(l_sc.shape, F32)

    qs = (qm_sc[...] * ATT_SCALE).astype(BF16)
    lane_b = lax.broadcasted_iota(jnp.int32, (rows, LANES), 1)
    kl = lax.broadcasted_iota(jnp.int32, (rows, MOBA_BLOCK), 1).astype(F32)

    scores = [jnp.dot(qs, kp[r][...].astype(BF16), preferred_element_type=F32) for r in range(ppc)]
    g_new, m_new, l_new = g_sc[...], m_sc[...], l_sc[...]
    probs = []
    for bl in range(bpc):
        gb = c * bpc + bl
        s = jnp.concatenate(scores[bl * ppb:(bl + 1) * ppb], axis=-1)
        gsum = jnp.sum(s, axis=-1, keepdims=True)
        s = s - slope * (posq - (gb * MOBA_BLOCK).astype(F32) - kl)
        mb = jnp.max(s, axis=-1, keepdims=True)
        p = jnp.exp(s - mb)
        lb = jnp.sum(p, axis=-1, keepdims=True)
        probs.append(p.astype(BF16))
        g_new = jnp.where(lane_b == gb, gsum, g_new)
        m_new = jnp.where(lane_b == gb, mb, m_new)
        l_new = jnp.where(lane_b == gb, lb, l_new)
    g_sc[...] = g_new
    m_sc[...] = m_new
    l_sc[...] = l_new
    for bl in range(bpc):
        ob = None
        for n_ in range(ppb):
            pr = probs[bl][:, n_ * PAGE_SIZE:(n_ + 1) * PAGE_SIZE]
            part = lax.dot_general(pr, vp[bl * ppb + n_][...].astype(BF16), _NT, preferred_element_type=F32)
            ob = part if ob is None else ob + part
        o_sc[c * bpc + bl] = ob

    @pl.when(c == pl.num_programs(1) - 1)
    def _():
        valid = lane_b < n_blocks
        gate = jnp.where(valid, g_sc[...], -jnp.inf)
        rank = jnp.zeros((rows, LANES), jnp.int32)
        for b in range(n_blocks):
            gcol = gate[:, b:b + 1]
            rank = rank + ((gcol > gate) | ((gcol == gate) & (b < lane_b))).astype(jnp.int32)
        sel = valid & (rank < MOBA_TOPK)
        kn = kn_ref[...]
        vn = vn_ref[...]
        tp = kn.shape[0]
        s_own = lax.dot_general(qs, kn.astype(BF16), _NT, preferred_element_type=F32)
        tk = lax.broadcasted_iota(jnp.int32, (rows, tp), 1)
        tq = lax.broadcasted_iota(jnp.int32, (rows, tp), 0) % t
        s_own = s_own - slope * (tq - tk).astype(F32)
        s_own = jnp.where(tk <= tq, s_own, NEG_BIG)
        m_blk = jnp.where(sel, m_sc[...], NEG_BIG)
        m = jnp.maximum(jnp.max(m_blk, axis=-1, keepdims=True), jnp.max(s_own, axis=-1, keepdims=True))
        w = jnp.where(sel, jnp.exp(m_blk - m), 0.0)
        p_own = jnp.exp(s_own - m)
        den = jnp.sum(w * l_sc[...], axis=-1, keepdims=True) + jnp.sum(p_own, axis=-1, keepdims=True)
        num = jnp.dot(p_own.astype(BF16), vn.astype(BF16), preferred_element_type=F32)
        for b in range(n_blocks):
            num = num + w[:, b:b + 1] * o_sc[b]
        out = num / den
        r2 = lax.broadcasted_iota(jnp.int32, out.shape, 0)
        l2 = lax.broadcasted_iota(jnp.int32, out.shape, 1)
        out = jnp.where(r2 // t == l2 // HEAD_DIM, out, 0.0)
        pick = (lax.broadcasted_iota(jnp.int32, (tp, rows), 1) % t
                == lax.broadcasted_iota(jnp.int32, (tp, rows), 0)).astype(F32)
        o_ref[...] = jnp.dot(pick, out, precision=HIGHEST, preferred_element_type=F32)


def _cache_pages_t(cache):
    d, n_pool = cache.shape[:2]
    return jnp.transpose(cache, (0, 1, 3, 4, 2)).reshape(d, n_pool, ATT_W, PAGE_SIZE)


def _sample_attn(q, k_new, v_new, ck_t, cv_t, page_table, layer, *, ppc=32):
    n, t, _ = q.shape
    n_pages = page_table.shape[1]
    assert (n_pages * PAGE_SIZE) % MOBA_BLOCK == 0, "cached length must be whole MoBA blocks"
    assert n_pages % ppc == 0 and (ppc * PAGE_SIZE) % MOBA_BLOCK == 0
    n_blocks = n_pages * PAGE_SIZE // MOBA_BLOCK
    assert n_blocks <= LANES
    rows = N_HEADS * t
    tp = -(-t // SUBLANES) * SUBLANES
    q_rep = jnp.tile(q, (1, N_HEADS, 1))
    pad_t = lambda a: jnp.pad(a, ((0, 0), (0, tp - t), (0, 0)))

    def page_spec(r):
        return pl.BlockSpec((None, None, ATT_W, PAGE_SIZE),
                            lambda a, c, pt: (layer, pt[a * n_pages + c * ppc + r], 0, 0))

    tok_spec = lambda r: pl.BlockSpec((None, r, ATT_W), lambda a, c, pt: (a, 0, 0))
    grid_spec = pltpu.PrefetchScalarGridSpec(
        num_scalar_prefetch=1,
        grid=(n, n_pages // ppc),
        in_specs=[tok_spec(rows), tok_spec(tp), tok_spec(tp)] + [page_spec(r) for r in range(ppc)] * 2,
        out_specs=tok_spec(tp),
        scratch_shapes=[pltpu.VMEM((rows, ATT_W), F32), pltpu.VMEM((rows, LANES), F32),
                        pltpu.VMEM((rows, LANES), F32), pltpu.VMEM((rows, LANES), F32),
                        pltpu.VMEM((n_blocks, rows, ATT_W), F32)],
    )
    out = pl.pallas_call(
        functools.partial(_sample_attn_kernel, t=t, ppc=ppc, n_blocks=n_blocks),
        grid_spec=grid_spec,
        out_shape=jax.ShapeDtypeStruct((n, tp, ATT_W), F32),
        compiler_params=_cparams("arbitrary", "arbitrary"),
        name="moba_sample",
    )(page_table.reshape(-1), q_rep, pad_t(k_new), pad_t(v_new), *([ck_t] * ppc), *([cv_t] * ppc))
    return out[:, :t]


_CONV_PAD = SUBLANES
_POOL_PAD = 2 * SUBLANES


def _mixer_kernel(xbc_ref, dt_ref, z_ref, u_ref, cinit_ref, pinit_ref, sinit_ref,
                  cw_ref, cb_ref, dtb_ref, alg_ref, dsk_ref, sg_ref, pw_ref, ps_ref,
                  y_ref, po_ref, sfin_ref,
                  cext, pext, a1, a2, a3, st_sc, *stage, L, pos0, valid_len):
    c = pl.program_id(1)
    short = valid_len < L

    def chunk_rows(ref, stage_ref):
        if not short:
            return ref[...]
        stage_ref[...] = jnp.zeros(stage_ref.shape, F32)
        stage_ref[0:valid_len] = ref[...]
        return stage_ref[...]

    @pl.when(c == 0)
    def _():
        cext[0:_CONV_PAD] = cinit_ref[...]
        pext[0:_POOL_PAD] = pinit_ref[...]
        st_sc[...] = sinit_ref[...].reshape(SSD_W, D_STATE)

    row = lax.broadcasted_iota(jnp.int32, (L, 1), 0)

    if short:
        cext[_CONV_PAD:_CONV_PAD + L] = jnp.zeros((L, CONV_DIM), F32)
        cext[_CONV_PAD:_CONV_PAD + valid_len] = xbc_ref[...]
    else:
        cext[_CONV_PAD:_CONV_PAD + L] = xbc_ref[...]
    conv = cb_ref[...] + cext[_CONV_PAD - 3:_CONV_PAD - 3 + L] * cw_ref[0:1, :]
    for i in range(1, CONV_W):
        conv = conv + cext[_CONV_PAD - 3 + i:_CONV_PAD - 3 + i + L] * cw_ref[i:i + 1, :]
    cext[0:_CONV_PAD] = cext[L:L + _CONV_PAD]
    conv = _silu(conv)
    xs = conv[:, 0:SSD_W]
    bm = conv[:, SSD_W:SSD_W + SSD_BC]
    cm = conv[:, SSD_W + SSD_BC:]

    dt = _softplus(chunk_rows(dt_ref, stage[0] if short else None) + dtb_ref[...])
    if short:
        dt = jnp.where(row < valid_len, dt, 0.0)
    a = -jnp.exp(alg_ref[...])
    da = dt * a
    ri = lax.broadcasted_iota(jnp.int32, (L, L), 0)
    ci = lax.broadcasted_iota(jnp.int32, (L, L), 1)
    causal = ci <= ri
    tril = causal.astype(F32)
    acum = jnp.dot(tril, da, precision=HIGHEST, preferred_element_type=F32)
    acum_t = acum.T
    a_last = acum[L - 1:L, :]

    lane_head = lax.broadcasted_iota(jnp.int32, (1, SSD_W), 1) // SSD_HEADDIM
    dt_b = jnp.zeros((L, SSD_W), F32)
    for h in range(SSD_HEADS):
        dt_b = jnp.where(lane_head == h, dt[:, h:h + 1], dt_b)
    xd = xs * dt_b
    xd_t = xd.T

    state = st_sc[...]
    y = jnp.zeros((L, SSD_W), F32)
    new_state = []
    for g in range(SSD_GROUPS):
        bg = bm[:, g * D_STATE:(g + 1) * D_STATE]
        cg = cm[:, g * D_STATE:(g + 1) * D_STATE]
        cb = lax.dot_general(cg, bg, _NT, preferred_element_type=F32)
        for hh in range(HEADS_PER_GROUP):
            h = g * HEADS_PER_GROUP + hh
            col = acum[:, h:h + 1]
            diff = col - acum_t[h:h + 1, :]
            lmat = jnp.exp(jnp.where(causal, diff, NEG_BIG))
            y_diag = jnp.dot(cb * lmat, xd, preferred_element_type=F32)
            y_off = lax.dot_general(cg * jnp.exp(col), state, _NT, preferred_element_type=F32)
            y = jnp.where(lane_head == h, y_diag + y_off, y)
            decay = jnp.exp(a_last[:, h:h + 1] - col)
            st = jnp.dot(xd_t, bg * decay, preferred_element_type=F32)
            lo = h * SSD_HEADDIM
            new_state.append(jnp.exp(a_last[:, h:h + 1]) * state[lo:lo + SSD_HEADDIM]
                             + st[lo:lo + SSD_HEADDIM])
    state = jnp.concatenate(new_state, axis=0)
    st_sc[...] = state

    y = y + dsk_ref[...] * xs
    yz = y * _silu(chunk_rows(z_ref, stage[1] if short else None))
    y_out = yz * lax.rsqrt(jnp.mean(yz * yz, axis=-1, keepdims=True) + RMS_EPS) * sg_ref[...]
    y_ref[...] = y_out[:valid_len] if short else y_out

    @pl.when(c == pl.num_programs(1) - 1)
    def _():
        sfin_ref[...] = state.reshape(SSD_HEADS, SSD_HEADDIM, D_STATE)

    P = _POOL_PAD
    if short:
        pext[P:P + L] = jnp.zeros((L, POOL_W), F32)
        pext[P:P + valid_len] = u_ref[...]
    else:
        pext[P:P + L] = u_ref[...]
    u = pext[P:P + L]
    n_ext = P + L
    a1[1:n_ext] = pext[1:n_ext] + pext[0:n_ext - 1]
    a2[3:n_ext] = a1[3:n_ext] + a1[1:n_ext - 2]
    a3[7:n_ext] = a2[7:n_ext] + a2[3:n_ext - 4]
    s16 = a3[P:n_ext] + a3[P - 8:n_ext - 8]
    pext[0:P] = pext[L:L + P]
    pos1 = pos0 + c * L + row + 1
    lane_grp = lax.broadcasted_iota(jnp.int32, (1, POOL_W), 1) // POOL_GROUP_W
    sums = (a1[P:n_ext], a2[P:n_ext], a3[P:n_ext], s16)
    win = s16
    width = jnp.full((1, POOL_W), POOL_WINDOWS[-1], jnp.int32)
    for gi, w in enumerate(POOL_WINDOWS[:-1]):
        win = jnp.where(lane_grp == gi, sums[gi], win)
        width = jnp.where(lane_grp == gi, w, width)
    cnt = jnp.minimum(width, pos1).astype(F32)
    d = win * (1.0 / cnt) - u
    po = jnp.dot(d.astype(BF16), pw_ref[...], preferred_element_type=F32) * ps_ref[...]
    po_ref[...] = po[:valid_len] if short else po


def _mixer(xbc, dt, z, u, conv_init, pool_init, ssm_init, cw, cb, dtb, alg, dsk, sg, pw_bd, ps,
           *, pos0):
    n, t, _ = xbc.shape
    L = SSD_CHUNK
    assert t % L == 0 or t < L, "whole chunks, or one short chunk"
    valid_len = min(t, L)
    seq = lambda w: pl.BlockSpec((None, valid_len, w), lambda a, c: (a, c, 0))
    per_seq = lambda *s: pl.BlockSpec((None,) + s, lambda a, c: (a,) + (0,) * len(s))
    return pl.pallas_call(
        functools.partial(_mixer_kernel, L=L, pos0=pos0, valid_len=valid_len),
        grid=(n, -(-t // L)),
        in_specs=[seq(CONV_DIM), seq(DT_PAD), seq(SSD_W), seq(POOL_W),
                  per_seq(_CONV_PAD, CONV_DIM), per_seq(_POOL_PAD, POOL_W),
                  per_seq(SSD_HEADS, SSD_HEADDIM, D_STATE),
                  _const_spec((CONV_W, CONV_DIM)), _const_spec((1, CONV_DIM)), _const_spec((1, DT_PAD)),
                  _const_spec((1, DT_PAD)), _const_spec((1, SSD_W)), _const_spec((1, SSD_W)),
                  _const_spec((POOL_W, POOL_W)), _const_spec((1, POOL_W))],
        out_specs=[seq(SSD_W), seq(POOL_W), per_seq(SSD_HEADS, SSD_HEADDIM, D_STATE)],
        out_shape=[jax.ShapeDtypeStruct((n, t, SSD_W), F32), jax.ShapeDtypeStruct((n, t, POOL_W), F32),
                   jax.ShapeDtypeStruct((n, SSD_HEADS, SSD_HEADDIM, D_STATE), F32)],
        scratch_shapes=[pltpu.VMEM((_CONV_PAD + L, CONV_DIM), F32), pltpu.VMEM((_POOL_PAD + L, POOL_W), F32),
                        pltpu.VMEM((_POOL_PAD + L, POOL_W), F32), pltpu.VMEM((_POOL_PAD + L, POOL_W), F32),
                        pltpu.VMEM((_POOL_PAD + L, POOL_W), F32), pltpu.VMEM((SSD_W, D_STATE), F32)]
                       + ([pltpu.VMEM((L, DT_PAD), F32), pltpu.VMEM((L, SSD_W), F32)] if valid_len < L else []),
        compiler_params=_cparams("arbitrary", "arbitrary"),
        name="mixer",
    )(xbc, dt, z, u, conv_init, pool_init, ssm_init, cw, cb, dtb, alg, dsk, sg, pw_bd, ps)


_FF_CHUNK = 256


def _outffn_kernel(x_ref, att_ref, y_ref, p_ref, wo_ref, g2_ref, wgu_ref, wdn_ref, o_ref, *, d_ff):
    mix = jnp.concatenate([att_ref[...], y_ref[...], p_ref[...]], axis=-1).astype(BF16)
    h = x_ref[...] + jnp.dot(mix, wo_ref[...], preferred_element_type=F32)
    hn = (h * lax.rsqrt(jnp.mean(h * h, axis=-1, keepdims=True) + RMS_EPS) * g2_ref[...]).astype(BF16)
    acc = h
    for c0 in range(0, d_ff, _FF_CHUNK):
        gate = jnp.dot(hn, wgu_ref[:, c0:c0 + _FF_CHUNK], preferred_element_type=F32)
        up = jnp.dot(hn, wgu_ref[:, d_ff + c0:d_ff + c0 + _FF_CHUNK], preferred_element_type=F32)
        ff = (_silu(gate) * up).astype(BF16)
        acc = acc + jnp.dot(ff, wdn_ref[c0:c0 + _FF_CHUNK, :], preferred_element_type=F32)
    o_ref[...] = acc


def _outffn(x2d, att, y, pool, wo, g2, wgu, wdn, *, tm):
    rows = x2d.shape[0]
    d_ff = wdn.shape[0]
    assert rows % tm == 0 and d_ff % _FF_CHUNK == 0
    row_spec = lambda w: pl.BlockSpec((tm, w), lambda i: (i, 0))
    return pl.pallas_call(
        functools.partial(_outffn_kernel, d_ff=d_ff),
        grid=(rows // tm,),
        in_specs=[row_spec(D_MODEL), row_spec(ATT_W), row_spec(SSD_W), row_spec(POOL_W),
                  _const_spec((D_MODEL, D_MODEL)), _const_spec((1, D_MODEL)),
                  _const_spec((D_MODEL, 2 * d_ff)), _const_spec((d_ff, D_MODEL))],
        out_specs=row_spec(D_MODEL),
        out_shape=jax.ShapeDtypeStruct((rows, D_MODEL), F32),
        compiler_params=_cparams("arbitrary"),
        name="outffn",
    )(x2d, att, y, pool, wo, g2, wgu, wdn)


def _prep_layer_params(norm1, w_in, q_norm, k_norm, conv_w, conv_b, dt_bias, a_log, d_skip, ssd_norm,
                       pool_w, pool_scale, w_out, norm2, w_gu, w_down):
    s = [ATT_W, 2 * ATT_W, 3 * ATT_W, 3 * ATT_W + SSD_W, 3 * ATT_W + SSD_W + CONV_DIM,
         3 * ATT_W + SSD_W + CONV_DIM + SSD_HEADS]
    wq, wk, wv, wz, wx, wdt, wu = jnp.split(jnp.transpose(w_in), s, axis=0)
    wdt = jnp.pad(wdt, ((0, DT_PAD - SSD_HEADS), (0, 0)))
    w_t = jnp.concatenate([wq, wk, wv, wz, wx, wu, wdt], axis=0).astype(BF16)
    pad_h = lambda v: jnp.pad(v.astype(F32), (0, DT_PAD - SSD_HEADS)).reshape(1, DT_PAD)
    pw_bd = jnp.zeros((POOL_W, POOL_W), F32)
    for gi in range(len(POOL_WINDOWS)):
        sl = slice(gi * POOL_GROUP_W, (gi + 1) * POOL_GROUP_W)
        pw_bd = pw_bd.at[sl, sl].set(pool_w[gi].astype(F32))
    return dict(
        g1=norm1.reshape(1, D_MODEL), w_t=w_t,
        qg=jnp.tile(q_norm, N_HEADS).reshape(1, ATT_W), kg=jnp.tile(k_norm, N_HEADS).reshape(1, ATT_W),
        cw=conv_w, cb=conv_b.reshape(1, CONV_DIM), dtb=pad_h(dt_bias), alg=pad_h(a_log),
        dsk=jnp.repeat(d_skip.astype(F32), SSD_HEADDIM).reshape(1, SSD_W), sg=ssd_norm.reshape(1, SSD_W),
        pw_bd=pw_bd.astype(BF16), ps=pool_scale.reshape(1, POOL_W),
        wo=w_out.astype(BF16), g2=norm2.reshape(1, D_MODEL), wgu=w_gu.astype(BF16), wdn=w_down.astype(BF16))


def _head_mean_matrix():
    r = jnp.arange(ATT_W) // HEAD_DIM
    return jnp.where(r[:, None] == r[None, :], 1.0 / HEAD_DIM, 0.0).astype(BF16)


def _layer(x, p, e_mat, conv_prev, ssm_prev, pool_prev, attend, *, pos0, tm, is_prompt, kv_stack=None, layer=0):
    n, t, _ = x.shape
    x2d = x.reshape(n * t, D_MODEL)
    pr = _inproj(x2d, p["g1"], p["w_t"], p["qg"], p["kg"], e_mat, tm=tm, seq_len=t if is_prompt else None,
                 kv_stack=kv_stack, layer=layer)
    seq = lambda a: a.reshape(n, t, a.shape[-1])
    q, z, xbc, u, dt = [seq(pr[name]) for name in ("q", "z", "xbc", "u", "dt")]
    if is_prompt:
        kmean = pr["kmean"].reshape(n, t // MOBA_BLOCK, ATT_W)
        att = attend(q, seq(pr["k_aug"]), seq(pr["v_aug"]), kmean)
        k_out, v_out = pr["kt"], pr["vt"]
    else:
        att = attend(q, seq(pr["k"]), seq(pr["v"]), None)
        k_out, v_out = [pr[name].reshape(n, t, N_HEADS, HEAD_DIM) for name in ("k", "v")]

    conv_init = jnp.pad(conv_prev, ((0, 0), (_CONV_PAD - (CONV_W - 1), 0), (0, 0)))
    pool_init = jnp.pad(pool_prev, ((0, 0), (_POOL_PAD - POOL_HIST, 0), (0, 0)))
    y, pool, ssm_new = _mixer(xbc, dt, z, u, conv_init, pool_init, ssm_prev,
                              p["cw"], p["cb"], p["dtb"], p["alg"], p["dsk"], p["sg"], p["pw_bd"], p["ps"], pos0=pos0)
    y = y.reshape(n * t, SSD_W)
    pool = pool.reshape(n * t, POOL_W)

    out = _outffn(x2d, att.reshape(n * t, ATT_W), y, pool, p["wo"], p["g2"], p["wgu"], p["wdn"], tm=tm)
    conv_state = jnp.concatenate([conv_prev, xbc], axis=1)[:, -(CONV_W - 1):]
    pool_state = jnp.concatenate([pool_prev, u], axis=1)[:, -POOL_HIST:]
    return out.reshape(n, t, D_MODEL), k_out, v_out, conv_state, ssm_new, pool_state


def kernel(x_prompt, x_sample, cache_k, cache_v, page_table, state_ssm, state_conv, state_pool, norm1, w_in, q_norm, k_norm, conv_w, conv_b, dt_bias, a_log, d_skip, ssd_norm, pool_w, pool_scale, w_out, norm2, w_gu, w_down):
    depth = w_in.shape[0]
    b, s, _ = x_prompt.shape
    nd, ts, _ = x_sample.shape
    past_len = page_table.shape[1] * PAGE_SIZE
    e_mat = _head_mean_matrix()
    conv0 = jnp.zeros((b, CONV_W - 1, CONV_DIM), x_prompt.dtype)
    ssm0 = jnp.zeros((b, SSD_HEADS, SSD_HEADDIM, D_STATE), F32)
    pool0 = jnp.zeros((b, POOL_HIST, POOL_W), x_prompt.dtype)
    ck_t, cv_t = _cache_pages_t(cache_k), _cache_pages_t(cache_v)
    yp, ys = x_prompt, x_sample
    kv_stack = [jnp.zeros((depth, b, ATT_W, s), F32) for _ in range(2)]
    acc = [[] for _ in range(8)]
    for l in range(depth):
        p = _prep_layer_params(norm1[l], w_in[l], q_norm[l], k_norm[l], conv_w[l], conv_b[l], dt_bias[l],
                               a_log[l], d_skip[l], ssd_norm[l], pool_w[l], pool_scale[l], w_out[l],
                               norm2[l], w_gu[l], w_down[l])
        attend_p = lambda q, k, v, kmean: _prompt_attn(q, k, v, kmean)
        yp, *kv_stack, cp, hp, pp = _layer(yp, p, e_mat, conv0, ssm0, pool0, attend_p,
                                           pos0=0, tm=512, is_prompt=True, kv_stack=kv_stack, layer=l)
        attend_s = lambda q, k, v, kmean, l=l: _sample_attn(q, k, v, ck_t, cv_t, page_table, l)
        ys, kk, vv, cc, hh, pq = _layer(ys, p, e_mat, state_conv[l], state_ssm[l], state_pool[l], attend_s,
                                        pos0=past_len, tm=nd * ts, is_prompt=False)
        for lst, val in zip(acc, (kk, vv, hp, hh, cp, cc, pp, pq)):
            lst.append(val)
    kp, vp = [jnp.transpose(a.reshape(depth, b, N_HEADS, HEAD_DIM, s), (0, 1, 4, 2, 3)) for a in kv_stack]
    return (yp, ys, kp, vp) + tuple(jnp.stack(a) for a in acc)
```

```python
import functools
import math

import jax
import jax.numpy as jnp
from jax import lax
from jax.experimental import pallas as pl
from jax.experimental.pallas import tpu as pltpu

D_MODEL = 1024
ATT_W = 512
SSD_W = 256
POOL_W = 256
HEAD_DIM = 64
N_HEADS = ATT_W // HEAD_DIM
MOBA_BLOCK = 256
MOBA_TOPK = 3
ATT_SCALE = HEAD_DIM ** -0.5
SSD_HEADDIM = 64
SSD_HEADS = SSD_W // SSD_HEADDIM
SSD_GROUPS = 2
HEADS_PER_GROUP = SSD_HEADS // SSD_GROUPS
D_STATE = 128
SSD_BC = SSD_GROUPS * D_STATE
CONV_W = 4
CONV_DIM = SSD_W + 2 * SSD_BC
SSD_CHUNK = 128
POOL_WINDOWS = (2, 4, 8, 16)
POOL_GROUP_W = POOL_W // len(POOL_WINDOWS)
POOL_HIST = max(POOL_WINDOWS) - 1
PAGE_SIZE = 128
RMS_EPS = 1e-6

LANES = 128
SUBLANES = 8
VMEM_LIMIT_BYTES = 56 * 1024 * 1024

NEG_BIG = -1e30
LOG2E = math.log2(math.e)
DT_PAD = LANES
IN_W_PAD = 3 * ATT_W + SSD_W + CONV_DIM + POOL_W + DT_PAD

F32 = jnp.float32
BF16 = jnp.bfloat16
HIGHEST = lax.Precision.HIGHEST

_NT = (((1,), (1,)), ((), ()))
_TN = (((0,), (0,)), ((), ()))


def _silu(x):
    return x * (0.5 * jnp.tanh(0.5 * x) + 0.5)


def _softplus(x):
    return jnp.maximum(x, 0.0) + jnp.log1p(jnp.exp(-jnp.abs(x)))


def _cparams(*sem, **kw):
    return pltpu.CompilerParams(dimension_semantics=sem, vmem_limit_bytes=VMEM_LIMIT_BYTES, **kw)


def _const_spec(shape):
    zeros = (0,) * len(shape)
    return pl.BlockSpec(shape, lambda *_: zeros, pipeline_mode=pl.Buffered(1))


def _inproj_kernel(x_ref, g1_ref, wt_ref, qg_ref, kg_ref, e_ref, *refs, n_blk):
    if n_blk:
        refs = refs[2:]
    q_ref, z_ref, xbc_ref, u_ref, dt_ref, *kv_refs = refs
    x = x_ref[...]
    xn = x * lax.rsqrt(jnp.mean(x * x, axis=-1, keepdims=True) + RMS_EPS) * g1_ref[...]
    y = lax.dot_general(xn.astype(BF16), wt_ref[...], _NT, preferred_element_type=F32)

    def head_norm(t, g):
        ms = jnp.dot((t * t).astype(BF16), e_ref[...], preferred_element_type=F32)
        return t * lax.rsqrt(ms + RMS_EPS) * g

    o = 0
    q_ref[...] = head_norm(y[:, o:o + ATT_W], qg_ref[...]); o += ATT_W
    kn = head_norm(y[:, o:o + ATT_W], kg_ref[...]); o += ATT_W
    vv = y[:, o:o + ATT_W]; o += ATT_W
    z_ref[...] = y[:, o:o + SSD_W]; o += SSD_W
    xbc_ref[...] = y[:, o:o + CONV_DIM]; o += CONV_DIM
    u_ref[...] = y[:, o:o + POOL_W]; o += POOL_W
    dt_ref[...] = y[:, o:o + DT_PAD]
    if n_blk == 0:
        k_ref, v_ref = kv_refs
        k_ref[...] = kn
        v_ref[...] = vv
    else:
        kb_ref, vb_ref, kt_ref, vt_ref, mean_ref = kv_refs
        tm = kn.shape[0]
        lane = lax.broadcasted_iota(jnp.int32, (tm, LANES), 1)
        kl = (lax.broadcasted_iota(jnp.int32, (tm, LANES), 0) % MOBA_BLOCK).astype(F32)
        k_slabs, v_slabs = [], []
        for c in range(0, ATT_W, LANES):
            for half in (0, 1):
                own = (lane // HEAD_DIM) == half
                in_aux = (lane >= _AUX_K_START[half]) & (lane < _AUX_K_START[half] + _AUX_K_LANES)
                k_slabs.append(jnp.where(own, kn[:, c:c + LANES], jnp.where(in_aux, kl, 0.0)).astype(BF16))
                v_slabs.append(jnp.where(own, vv[:, c:c + LANES],
                                         (lane == _ONES_LANE[half]).astype(F32)).astype(BF16))
        kb_ref[...] = jnp.concatenate(k_slabs, axis=-1)
        vb_ref[...] = jnp.concatenate(v_slabs, axis=-1)
        kt_ref[...] = kn.T
        vt_ref[...] = vv.T
        for b in range(n_blk):
            blk = kn[b * MOBA_BLOCK:(b + 1) * MOBA_BLOCK]
            mean_ref[b] = jnp.sum(blk, axis=0, keepdims=True) * (1.0 / MOBA_BLOCK)


def _inproj(x2d, g1, w_t, qg, kg, e_mat, *, tm, seq_len=None, kv_stack=None, layer=0):
    rows = x2d.shape[0]
    assert rows % tm == 0
    row_spec = lambda w: pl.BlockSpec((tm, w), lambda i: (i, 0))
    names = ["q", "z", "xbc", "u", "dt"]
    out_shape = [jax.ShapeDtypeStruct((rows, w), F32) for w in (ATT_W, SSD_W, CONV_DIM, POOL_W, DT_PAD)]
    out_specs = [row_spec(s.shape[1]) for s in out_shape]
    if seq_len is None:
        n_blk = 0
        names += ["k", "v"]
        out_shape += [jax.ShapeDtypeStruct((rows, ATT_W), F32)] * 2
        out_specs += [row_spec(ATT_W)] * 2
    else:
        assert tm % MOBA_BLOCK == 0 and seq_len % tm == 0 and rows % seq_len == 0
        n_blk = tm // MOBA_BLOCK
        tps = seq_len // tm
        names += ["k_aug", "v_aug", "kt", "vt", "kmean"]
        out_shape += [jax.ShapeDtypeStruct((rows, 2 * ATT_W), BF16)] * 2
        out_specs += [row_spec(2 * ATT_W)] * 2
        out_shape += [jax.ShapeDtypeStruct(a.shape, F32) for a in kv_stack]
        out_specs += [pl.BlockSpec((None, None, ATT_W, tm), lambda i: (layer, i // tps, 0, i % tps))] * 2
        out_shape.append(jax.ShapeDtypeStruct((rows // MOBA_BLOCK, 1, ATT_W), F32))
        out_specs.append(pl.BlockSpec((n_blk, 1, ATT_W), lambda i: (i, 0, 0)))
    in_specs = [row_spec(D_MODEL), _const_spec((1, D_MODEL)), _const_spec((IN_W_PAD, D_MODEL)),
                _const_spec((1, ATT_W)), _const_spec((1, ATT_W)), _const_spec((ATT_W, ATT_W))]
    args = [x2d, g1, w_t, qg, kg, e_mat]
    aliases = {}
    if seq_len is not None:
        aliases = {len(args) + n_: names.index(name) for n_, name in enumerate(("kt", "vt"))}
        in_specs += [pl.BlockSpec(memory_space=pl.ANY)] * 2
        args += list(kv_stack)
    outs = pl.pallas_call(
        functools.partial(_inproj_kernel, n_blk=n_blk),
        grid=(rows // tm,),
        in_specs=in_specs,
        out_specs=out_specs,
        out_shape=out_shape,
        input_output_aliases=aliases,
        compiler_params=_cparams("arbitrary"),
        name="inproj",
    )(*args)
    return dict(zip(names, outs))


_AUX_K_LANES = 3
_AUX_K_START = (HEAD_DIM, 0)
_ONES_LANE = (HEAD_DIM, HEAD_DIM - 1)
_ACC_ROWS = HEAD_DIM + SUBLANES
_ACC_ROW0 = (0, HEAD_DIM - SUBLANES)
_ATTN_Q_BLOCKS_PER_STEP = (16, 8, 4, 2, 1)


def _bf16_terms(x, n):
    terms = []
    for _ in range(n):
        t = x.astype(BF16).astype(F32)
        terms.append(t)
        x = x - t
    return terms


def _prompt_attn_kernel(q_ref, k_ref, v_ref, km_ref, o_ref, *scratch, nb, qb):
    per_set = len(scratch) // 2
    for t_ in range(qb):
        rows = pl.ds(t_ * MOBA_BLOCK, MOBA_BLOCK)
        scr = scratch[(t_ % 2) * per_set:(t_ % 2 + 1) * per_set]
        _prompt_attn_block(pl.program_id(2) * qb + t_, q_ref.at[rows], k_ref, v_ref, km_ref, o_ref.at[rows],
                           *scr, nb=nb)


def _prompt_attn_block(i, q_ref, k_ref, v_ref, km_ref, o_ref,
                       bias_sc, qs_sc, s_a, s_b, s_d, p_a, p_b, al_a, al_b, m_sc, acc_sc, *, nb):
    j = pl.program_id(1)
    blk = MOBA_BLOCK
    w2 = 2 * blk

    lane = lax.broadcasted_iota(jnp.int32, (1, w2), 1)
    head = 2 * j + (lane >= blk).astype(jnp.int32)
    slope2 = jnp.exp2(-(head + 1).astype(F32)) * LOG2E

    q2 = q_ref[...]
    lane_q = lax.broadcasted_iota(jnp.int32, q2.shape, 1)
    for half in (0, 1):
        own = (lane_q // HEAD_DIM) == half
        aux = jnp.zeros(q2.shape, F32)
        slope_h = jnp.exp2(-(2 * j + half + 1).astype(F32)) * LOG2E
        for n_, term in enumerate(_bf16_terms(jnp.full((1, 1), slope_h, F32), _AUX_K_LANES)):
            aux = jnp.where(lane_q == _AUX_K_START[half] + n_, term, aux)
        qs_sc[half] = jnp.where(own, q2 * (ATT_SCALE * LOG2E), aux).astype(BF16)

    def qk(b, dst):
        start = pl.multiple_of(b * blk, blk)
        for half in (0, 1):
            dst[:, half * blk:(half + 1) * blk] = lax.dot_general(
                k_ref[pl.ds(start, blk), half * LANES:(half + 1) * LANES], qs_sc[half], _NT,
                preferred_element_type=F32)

    def softmax(s, r, p_ref, al_ref):
        m_prev = m_sc[...]
        cmax = jnp.max(s, axis=0, keepdims=True)
        m_new = jnp.maximum(m_prev, cmax + r)
        p_ref[...] = jnp.exp2(s - jnp.maximum(m_new - r, cmax)).astype(BF16)
        al_ref[...] = jnp.exp2(m_prev - m_new)
        m_sc[...] = m_new

    qm = jnp.concatenate([jnp.where(lane_q < HEAD_DIM, q2, 0.0),
                          jnp.where(lane_q >= HEAD_DIM, q2, 0.0)], axis=0)
    km_hi, km_lo = [t.astype(BF16) for t in _bf16_terms(km_ref[...], 2)]
    qm_hi, qm_lo = [t.astype(BF16) for t in _bf16_terms(qm, 2)]
    ntdot = lambda a, b: lax.dot_general(a, b, _NT, preferred_element_type=F32)
    gate = ntdot(km_hi, qm_hi) + (ntdot(km_hi, qm_lo) + ntdot(km_lo, qm_hi))

    qk(i, s_d)
    qk(0, s_a)
    m_sc[...] = jnp.full(m_sc.shape, NEG_BIG, F32)
    kl = lax.broadcasted_iota(jnp.int32, (blk, w2), 0)
    ql = lax.broadcasted_iota(jnp.int32, (blk, w2), 1) % blk
    softmax(jnp.where(kl <= ql, s_d[...], NEG_BIG), 0.0, p_b, al_b)

    bidx = lax.broadcasted_iota(jnp.int32, (nb, w2), 0)
    past = bidx < i
    gate = jnp.where(past, gate, -jnp.inf)
    rank = jnp.zeros((nb, w2), jnp.int32)
    for b in range(nb):
        gb = gate[b:b + 1, :]
        rank = rank + ((gb > gate) | ((gb == gate) & (b < bidx))).astype(jnp.int32)
    sel = past & (rank < MOBA_TOPK)
    bias_sc[...] = jnp.where(sel, -slope2 * ((i - bidx) * blk).astype(F32), NEG_BIG)

    def pv_acc(b, p_ref, al_ref):
        start = pl.multiple_of(b * blk, blk)
        for half in (0, 1):
            cols = slice(half * blk, (half + 1) * blk)
            pv = lax.dot_general(v_ref[pl.ds(start, blk), half * LANES:(half + 1) * LANES], p_ref[:, cols], _TN,
                                 preferred_element_type=F32)
            r0 = _ACC_ROW0[half]
            acc_sc[half] = al_ref[:, cols] * acc_sc[half] + pv[r0:r0 + _ACC_ROWS]

    def softmax_past(b, s_ref, p_ref, al_ref):
        softmax(s_ref[...], bias_sc[pl.ds(b, 1), :], p_ref, al_ref)

    acc_sc[...] = jnp.zeros(acc_sc.shape, F32)

    def body(p, carry):
        a = 2 * p
        softmax_past(a, s_a, p_a, al_a)
        qk(a + 1, s_b)
        pv_acc(jnp.where(p == 0, i, a - 1), p_b, al_b)
        softmax_past(a + 1, s_b, p_b, al_b)
        qk(jnp.minimum(a + 2, nb - 1), s_a)
        pv_acc(a, p_a, al_a)
        return carry

    n_pairs = (i + 1) // 2
    lax.fori_loop(0, n_pairs, body, 0)
    pv_acc(jnp.where(n_pairs == 0, i, 2 * n_pairs - 1), p_b, al_b)

    lo, hi = acc_sc[0], acc_sc[1]
    o2t = jnp.concatenate([lo[:HEAD_DIM] / lo[HEAD_DIM:HEAD_DIM + 1],
                           hi[SUBLANES:] / hi[SUBLANES - 1:SUBLANES]], axis=0)
    o_ref[...] = o2t.T


def _prompt_attn(q, k_aug, v_aug, kmean):
    n, t, _ = q.shape
    nb = t // MOBA_BLOCK
    assert t % MOBA_BLOCK == 0 and nb % 2 == 0 and kmean.shape == (n, nb, ATT_W)
    pairs = ATT_W // LANES
    blk = MOBA_BLOCK
    w2 = 2 * blk
    qb = max(d for d in _ATTN_Q_BLOCKS_PER_STEP if nb % d == 0)
    slab = pl.BlockSpec((None, t, 2 * LANES), lambda a, j, i: (a, 0, j))
    scratch_set = ([pltpu.VMEM((nb, w2), F32), pltpu.VMEM((2, blk, LANES), BF16)]
                   + [pltpu.VMEM((blk, w2), F32)] * 3 + [pltpu.VMEM((blk, w2), BF16)] * 2
                   + [pltpu.VMEM((1, w2), F32)] * 3 + [pltpu.VMEM((2, _ACC_ROWS, blk), F32)])
    return pl.pallas_call(
        functools.partial(_prompt_attn_kernel, nb=nb, qb=qb),
        grid=(n, pairs, nb // qb),
        in_specs=[pl.BlockSpec((None, qb * blk, LANES), lambda a, j, i: (a, i, j)), slab, slab,
                  pl.BlockSpec((None, nb, LANES), lambda a, j, i: (a, 0, j))],
        out_specs=pl.BlockSpec((None, qb * blk, LANES), lambda a, j, i: (a, i, j)),
        out_shape=jax.ShapeDtypeStruct((n, t, ATT_W), F32),
        scratch_shapes=scratch_set * 2,
        compiler_params=_cparams("arbitrary", "arbitrary", "arbitrary"),
        name="moba_prompt",
    )(q, k_aug, v_aug, kmean)


def _sample_attn_kernel(pt_ref, q_ref, kn_ref, vn_ref, *refs, t, ppc, n_blocks):
    del pt_ref
    kp = refs[:ppc]
    vp = refs[ppc:2 * ppc]
    o_ref = refs[2 * ppc]
    qm_sc, g_sc, m_sc, l_sc, o_sc = refs[2 * ppc + 1:]
    c = pl.program_id(1)
    rows = N_HEADS * t
    bpc = ppc * PAGE_SIZE // MOBA_BLOCK
    ppb = MOBA_BLOCK // PAGE_SIZE
    past_len = n_blocks * MOBA_BLOCK

    rid = lax.broadcasted_iota(jnp.int32, (rows, 1), 0)
    slope = jnp.exp2(-(rid // t + 1).astype(F32))
    posq = (past_len + rid % t).astype(F32)

    @pl.when(c == 0)
    def _():
        qt = q_ref[...]
        r2 = lax.broadcasted_iota(jnp.int32, qt.shape, 0)
        l2 = lax.broadcasted_iota(jnp.int32, qt.shape, 1)
        qm_sc[...] = jnp.where(r2 // t == l2 // HEAD_DIM, qt, 0.0)
        g_sc[...] = jnp.zeros(g_sc.shape, F32)
        m_sc[...] = jnp.full(m_sc.shape, NEG_BIG, F32)
        l_sc[...] = jnp.zeros(l_sc.shape, F32)

    qs = (qm_sc[...] * ATT_SCALE).astype(BF16)
    lane_b = lax.broadcasted_iota(jnp.int32, (rows, LANES), 1)
    kl = lax.broadcasted_iota(jnp.int32, (rows, MOBA_BLOCK), 1).astype(F32)

    scores = [jnp.dot(qs, kp[r][...].astype(BF16), preferred_element_type=F32) for r in range(ppc)]
    g_new, m_new, l_new = g_sc[...], m_sc[...], l_sc[...]
    probs = []
    for bl in range(bpc):
        gb = c * bpc + bl
        s = jnp.concatenate(scores[bl * ppb:(bl + 1) * ppb], axis=-1)
        gsum = jnp.sum(s, axis=-1, keepdims=True)
        s = s - slope * (posq - (gb * MOBA_BLOCK).astype(F32) - kl)
        mb = jnp.max(s, axis=-1, keepdims=True)
        p = jnp.exp(s - mb)
        lb = jnp.sum(p, axis=-1, keepdims=True)
        probs.append(p.astype(BF16))
        g_new = jnp.where(lane_b == gb, gsum, g_new)
        m_new = jnp.where(lane_b == gb, mb, m_new)
        l_new = jnp.where(lane_b == gb, lb, l_new)
    g_sc[...] = g_new
    m_sc[...] = m_new
    l_sc[...] = l_new
    for bl in range(bpc):
        ob = None
        for n_ in range(ppb):
            pr = probs[bl][:, n_ * PAGE_SIZE:(n_ + 1) * PAGE_SIZE]
            part = lax.dot_general(pr, vp[bl * ppb + n_][...].astype(BF16), _NT, preferred_element_type=F32)
            ob = part if ob is None else ob + part
        o_sc[c * bpc + bl] = ob

    @pl.when(c == pl.num_programs(1) - 1)
    def _():
        valid = lane_b < n_blocks
        gate = jnp.where(valid, g_sc[...], -jnp.inf)
        rank = jnp.zeros((rows, LANES), jnp.int32)
        for b in range(n_blocks):
            gcol = gate[:, b:b + 1]
            rank = rank + ((gcol > gate) | ((gcol == gate) & (b < lane_b))).astype(jnp.int32)
        sel = valid & (rank < MOBA_TOPK)
        kn = kn_ref[...]
        vn = vn_ref[...]
        tp = kn.shape[0]
        s_own = lax.dot_general(qs, kn.astype(BF16), _NT, preferred_element_type=F32)
        tk = lax.broadcasted_iota(jnp.int32, (rows, tp), 1)
        tq = lax.broadcasted_iota(jnp.int32, (rows, tp), 0) % t
        s_own = s_own - slope * (tq - tk).astype(F32)
        s_own = jnp.where(tk <= tq, s_own, NEG_BIG)
        m_blk = jnp.where(sel, m_sc[...], NEG_BIG)
        m = jnp.maximum(jnp.max(m_blk, axis=-1, keepdims=True), jnp.max(s_own, axis=-1, keepdims=True))
        w = jnp.where(sel, jnp.exp(m_blk - m), 0.0)
        p_own = jnp.exp(s_own - m)
        den = jnp.sum(w * l_sc[...], axis=-1, keepdims=True) + jnp.sum(p_own, axis=-1, keepdims=True)
        num = jnp.dot(p_own.astype(BF16), vn.astype(BF16), preferred_element_type=F32)
        for b in range(n_blocks):
            num = num + w[:, b:b + 1] * o_sc[b]
        out = num / den
        r2 = lax.broadcasted_iota(jnp.int32, out.shape, 0)
        l2 = lax.broadcasted_iota(jnp.int32, out.shape, 1)
        out = jnp.where(r2 // t == l2 // HEAD_DIM, out, 0.0)
        pick = (lax.broadcasted_iota(jnp.int32, (tp, rows), 1) % t
                == lax.broadcasted_iota(jnp.int32, (tp, rows), 0)).astype(F32)
        o_ref[...] = jnp.dot(pick, out, precision=HIGHEST, preferred_element_type=F32)


def _cache_pages_t(cache):
    d, n_pool = cache.shape[:2]
    return jnp.transpose(cache, (0, 1, 3, 4, 2)).reshape(d, n_pool, ATT_W, PAGE_SIZE)


def _sample_attn(q, k_new, v_new, ck_t, cv_t, page_table, layer, *, ppc=32):
    n, t, _ = q.shape
    n_pages = page_table.shape[1]
    assert (n_pages * PAGE_SIZE) % MOBA_BLOCK == 0, "cached length must be whole MoBA blocks"
    assert n_pages % ppc == 0 and (ppc * PAGE_SIZE) % MOBA_BLOCK == 0
    n_blocks = n_pages * PAGE_SIZE // MOBA_BLOCK
    assert n_blocks <= LANES
    rows = N_HEADS * t
    tp = -(-t // SUBLANES) * SUBLANES
    q_rep = jnp.tile(q, (1, N_HEADS, 1))
    pad_t = lambda a: jnp.pad(a, ((0, 0), (0, tp - t), (0, 0)))

    def page_spec(r):
        return pl.BlockSpec((None, None, ATT_W, PAGE_SIZE),
                            lambda a, c, pt: (layer, pt[a * n_pages + c * ppc + r], 0, 0))

    tok_spec = lambda r: pl.BlockSpec((None, r, ATT_W), lambda a, c, pt: (a, 0, 0))
    grid_spec = pltpu.PrefetchScalarGridSpec(
        num_scalar_prefetch=1,
        grid=(n, n_pages // ppc),
        in_specs=[tok_spec(rows), tok_spec(tp), tok_spec(tp)] + [page_spec(r) for r in range(ppc)] * 2,
        out_specs=tok_spec(tp),
        scratch_shapes=[pltpu.VMEM((rows, ATT_W), F32), pltpu.VMEM((rows, LANES), F32),
                        pltpu.VMEM((rows, LANES), F32), pltpu.VMEM((rows, LANES), F32),
                        pltpu.VMEM((n_blocks, rows, ATT_W), F32)],
    )
    out = pl.pallas_call(
        functools.partial(_sample_attn_kernel, t=t, ppc=ppc, n_blocks=n_blocks),
        grid_spec=grid_spec,
        out_shape=jax.ShapeDtypeStruct((n, tp, ATT_W), F32),
        compiler_params=_cparams("arbitrary", "arbitrary"),
        name="moba_sample",
    )(page_table.reshape(-1), q_rep, pad_t(k_new), pad_t(v_new), *([ck_t] * ppc), *([cv_t] * ppc))
    return out[:, :t]


_CONV_PAD = SUBLANES
_POOL_PAD = 2 * SUBLANES


def _mixer_kernel(xbc_ref, dt_ref, z_ref, u_ref, cinit_ref, pinit_ref, sinit_ref,
                  cw_ref, cb_ref, dtb_ref, alg_ref, dsk_ref, sg_ref, pw_ref, ps_ref,
                  y_ref, po_ref, sfin_ref,
                  cext, pext, a1, a2, a3, st_sc, *stage, L, pos0, valid_len):
    c = pl.program_id(1)
    short = valid_len < L

    def chunk_rows(ref, stage_ref):
        if not short:
            return ref[...]
        stage_ref[...] = jnp.zeros(stage_ref.shape, F32)
        stage_ref[0:valid_len] = ref[...]
        return stage_ref[...]

    @pl.when(c == 0)
    def _():
        cext[0:_CONV_PAD] = cinit_ref[...]
        pext[0:_POOL_PAD] = pinit_ref[...]
        st_sc[...] = sinit_ref[...].reshape(SSD_W, D_STATE)

    row = lax.broadcasted_iota(jnp.int32, (L, 1), 0)

    if short:
        cext[_CONV_PAD:_CONV_PAD + L] = jnp.zeros((L, CONV_DIM), F32)
        cext[_CONV_PAD:_CONV_PAD + valid_len] = xbc_ref[...]
    else:
        cext[_CONV_PAD:_CONV_PAD + L] = xbc_ref[...]
    conv = cb_ref[...] + cext[_CONV_PAD - 3:_CONV_PAD - 3 + L] * cw_ref[0:1, :]
    for i in range(1, CONV_W):
        conv = conv + cext[_CONV_PAD - 3 + i:_CONV_PAD - 3 + i + L] * cw_ref[i:i + 1, :]
    cext[0:_CONV_PAD] = cext[L:L + _CONV_PAD]
    conv = _silu(conv)
    xs = conv[:, 0:SSD_W]
    bm = conv[:, SSD_W:SSD_W + SSD_BC]
    cm = conv[:, SSD_W + SSD_BC:]

    dt = _softplus(chunk_rows(dt_ref, stage[0] if short else None) + dtb_ref[...])
    if short:
        dt = jnp.where(row < valid_len, dt, 0.0)
    a = -jnp.exp(alg_ref[...])
    da = dt * a
    ri = lax.broadcasted_iota(jnp.int32, (L, L), 0)
    ci = lax.broadcasted_iota(jnp.int32, (L, L), 1)
    causal = ci <= ri
    tril = causal.astype(F32)
    acum = jnp.dot(tril, da, precision=HIGHEST, preferred_element_type=F32)
    acum_t = acum.T
    a_last = acum[L - 1:L, :]

    lane_head = lax.broadcasted_iota(jnp.int32, (1, SSD_W), 1) // SSD_HEADDIM
    dt_b = jnp.zeros((L, SSD_W), F32)
    for h in range(SSD_HEADS):
        dt_b = jnp.where(lane_head == h, dt[:, h:h + 1], dt_b)
    xd = xs * dt_b
    xd_t = xd.T

    state = st_sc[...]
    y = jnp.zeros((L, SSD_W), F32)
    new_state = []
    for g in range(SSD_GROUPS):
        bg = bm[:, g * D_STATE:(g + 1) * D_STATE]
        cg = cm[:, g * D_STATE:(g + 1) * D_STATE]
        cb = lax.dot_general(cg, bg, _NT, preferred_element_type=F32)
        for hh in range(HEADS_PER_GROUP):
            h = g * HEADS_PER_GROUP + hh
            col = acum[:, h:h + 1]
            diff = col - acum_t[h:h + 1, :]
            lmat = jnp.exp(jnp.where(causal, diff, NEG_BIG))
            y_diag = jnp.dot(cb * lmat, xd, preferred_element_type=F32)
            y_off = lax.dot_general(cg * jnp.exp(col), state, _NT, preferred_element_type=F32)
            y = jnp.where(lane_head == h, y_diag + y_off, y)
            decay = jnp.exp(a_last[:, h:h + 1] - col)
            st = jnp.dot(xd_t, bg * decay, preferred_element_type=F32)
            lo = h * SSD_HEADDIM
            new_state.append(jnp.exp(a_last[:, h:h + 1]) * state[lo:lo + SSD_HEADDIM]
                             + st[lo:lo + SSD_HEADDIM])
    state = jnp.concatenate(new_state, axis=0)
    st_sc[...] = state

    y = y + dsk_ref[...] * xs
    yz = y * _silu(chunk_rows(z_ref, stage[1] if short else None))
    y_out = yz * lax.rsqrt(jnp.mean(yz * yz, axis=-1, keepdims=True) + RMS_EPS) * sg_ref[...]
    y_ref[...] = y_out[:valid_len] if short else y_out

    @pl.when(c == pl.num_programs(1) - 1)
    def _():
        sfin_ref[...] = state.reshape(SSD_HEADS, SSD_HEADDIM, D_STATE)

    P = _POOL_PAD
    if short:
        pext[P:P + L] = jnp.zeros((L, POOL_W), F32)
        pext[P:P + valid_len] = u_ref[...]
    else:
        pext[P:P + L] = u_ref[...]
    u = pext[P:P + L]
    n_ext = P + L
    a1[1:n_ext] = pext[1:n_ext] + pext[0:n_ext - 1]
    a2[3:n_ext] = a1[3:n_ext] + a1[1:n_ext - 2]
    a3[7:n_ext] = a2[7:n_ext] + a2[3:n_ext - 4]
    s16 = a3[P:n_ext] + a3[P - 8:n_ext - 8]
    pext[0:P] = pext[L:L + P]
    pos1 = pos0 + c * L + row + 1
    lane_grp = lax.broadcasted_iota(jnp.int32, (1, POOL_W), 1) // POOL_GROUP_W
    sums = (a1[P:n_ext], a2[P:n_ext], a3[P:n_ext], s16)
    win = s16
    width = jnp.full((1, POOL_W), POOL_WINDOWS[-1], jnp.int32)
    for gi, w in enumerate(POOL_WINDOWS[:-1]):
        win = jnp.where(lane_grp == gi, sums[gi], win)
        width = jnp.where(lane_grp == gi, w, width)
    cnt = jnp.minimum(width, pos1).astype(F32)
    d = win * (1.0 / cnt) - u
    po = jnp.dot(d.astype(BF16), pw_ref[...], preferred_element_type=F32) * ps_ref[...]
    po_ref[...] = po[:valid_len] if short else po


def _mixer(xbc, dt, z, u, conv_init, pool_init, ssm_init, cw, cb, dtb, alg, dsk, sg, pw_bd, ps,
           *, pos0):
    n, t, _ = xbc.shape
    L = SSD_CHUNK
    assert t % L == 0 or t < L, "whole chunks, or one short chunk"
    valid_len = min(t, L)
    seq = lambda w: pl.BlockSpec((None, valid_len, w), lambda a, c: (a, c, 0))
    per_seq = lambda *s: pl.BlockSpec((None,) + s, lambda a, c: (a,) + (0,) * len(s))
    return pl.pallas_call(
        functools.partial(_mixer_kernel, L=L, pos0=pos0, valid_len=valid_len),
        grid=(n, -(-t // L)),
        in_specs=[seq(CONV_DIM), seq(DT_PAD), seq(SSD_W), seq(POOL_W),
                  per_seq(_CONV_PAD, CONV_DIM), per_seq(_POOL_PAD, POOL_W),
                  per_seq(SSD_HEADS, SSD_HEADDIM, D_STATE),
                  _const_spec((CONV_W, CONV_DIM)), _const_spec((1, CONV_DIM)), _const_spec((1, DT_PAD)),
                  _const_spec((1, DT_PAD)), _const_spec((1, SSD_W)), _const_spec((1, SSD_W)),
                  _const_spec((POOL_W, POOL_W)), _const_spec((1, POOL_W))],
        out_specs=[seq(SSD_W), seq(POOL_W), per_seq(SSD_HEADS, SSD_HEADDIM, D_STATE)],
        out_shape=[jax.ShapeDtypeStruct((n, t, SSD_W), F32), jax.ShapeDtypeStruct((n, t, POOL_W), F32),
                   jax.ShapeDtypeStruct((n, SSD_HEADS, SSD_HEADDIM, D_STATE), F32)],
        scratch_shapes=[pltpu.VMEM((_CONV_PAD + L, CONV_DIM), F32), pltpu.VMEM((_POOL_PAD + L, POOL_W), F32),
                        pltpu.VMEM((_POOL_PAD + L, POOL_W), F32), pltpu.VMEM((_POOL_PAD + L, POOL_W), F32),
                        pltpu.VMEM((_POOL_PAD + L, POOL_W), F32), pltpu.VMEM((SSD_W, D_STATE), F32)]
                       + ([pltpu.VMEM((L, DT_PAD), F32), pltpu.VMEM((L, SSD_W), F32)] if valid_len < L else []),
        compiler_params=_cparams("arbitrary", "arbitrary"),
        name="mixer",
    )(xbc, dt, z, u, conv_init, pool_init, ssm_init, cw, cb, dtb, alg, dsk, sg, pw_bd, ps)


_FF_CHUNK = 256


def _outffn_kernel(x_ref, att_ref, y_ref, p_ref, wo_ref, g2_ref, wgu_ref, wdn_ref, o_ref, *, d_ff):
    mix = jnp.concatenate([att_ref[...], y_ref[...], p_ref[...]], axis=-1).astype(BF16)
    h = x_ref[...] + jnp.dot(mix, wo_ref[...], preferred_element_type=F32)
    hn = (h * lax.rsqrt(jnp.mean(h * h, axis=-1, keepdims=True) + RMS_EPS) * g2_ref[...]).astype(BF16)
    acc = h
    for c0 in range(0, d_ff, _FF_CHUNK):
        gate = jnp.dot(hn, wgu_ref[:, c0:c0 + _FF_CHUNK], preferred_element_type=F32)
        up = jnp.dot(hn, wgu_ref[:, d_ff + c0:d_ff + c0 + _FF_CHUNK], preferred_element_type=F32)
        ff = (_silu(gate) * up).astype(BF16)
        acc = acc + jnp.dot(ff, wdn_ref[c0:c0 + _FF_CHUNK, :], preferred_element_type=F32)
    o_ref[...] = acc


def _outffn(x2d, att, y, pool, wo, g2, wgu, wdn, *, tm):
    rows = x2d.shape[0]
    d_ff = wdn.shape[0]
    assert rows % tm == 0 and d_ff % _FF_CHUNK == 0
    row_spec = lambda w: pl.BlockSpec((tm, w), lambda i: (i, 0))
    return pl.pallas_call(
        functools.partial(_outffn_kernel, d_ff=d_ff),
        grid=(rows // tm,),
        in_specs=[row_spec(D_MODEL), row_spec(ATT_W), row_spec(SSD_W), row_spec(POOL_W),
                  _const_spec((D_MODEL, D_MODEL)), _const_spec((1, D_MODEL)),
                  _const_spec((D_MODEL, 2 * d_ff)), _const_spec((d_ff, D_MODEL))],
        out_specs=row_spec(D_MODEL),
        out_shape=jax.ShapeDtypeStruct((rows, D_MODEL), F32),
        compiler_params=_cparams("arbitrary"),
        name="outffn",
    )(x2d, att, y, pool, wo, g2, wgu, wdn)


def _prep_layer_params(norm1, w_in, q_norm, k_norm, conv_w, conv_b, dt_bias, a_log, d_skip, ssd_norm,
                       pool_w, pool_scale, w_out, norm2, w_gu, w_down):
    s = [ATT_W, 2 * ATT_W, 3 * ATT_W, 3 * ATT_W + SSD_W, 3 * ATT_W + SSD_W + CONV_DIM,
         3 * ATT_W + SSD_W + CONV_DIM + SSD_HEADS]
    wq, wk, wv, wz, wx, wdt, wu = jnp.split(jnp.transpose(w_in), s, axis=0)
    wdt = jnp.pad(wdt, ((0, DT_PAD - SSD_HEADS), (0, 0)))
    w_t = jnp.concatenate([wq, wk, wv, wz, wx, wu, wdt], axis=0).astype(BF16)
    pad_h = lambda v: jnp.pad(v.astype(F32), (0, DT_PAD - SSD_HEADS)).reshape(1, DT_PAD)
    pw_bd = jnp.zeros((POOL_W, POOL_W), F32)
    for gi in range(len(POOL_WINDOWS)):
        sl = slice(gi * POOL_GROUP_W, (gi + 1) * POOL_GROUP_W)
        pw_bd = pw_bd.at[sl, sl].set(pool_w[gi].astype(F32))
    return dict(
        g1=norm1.reshape(1, D_MODEL), w_t=w_t,
        qg=jnp.tile(q_norm, N_HEADS).reshape(1, ATT_W), kg=jnp.tile(k_norm, N_HEADS).reshape(1, ATT_W),
        cw=conv_w, cb=conv_b.reshape(1, CONV_DIM), dtb=pad_h(dt_bias), alg=pad_h(a_log),
        dsk=jnp.repeat(d_skip.astype(F32), SSD_HEADDIM).reshape(1, SSD_W), sg=ssd_norm.reshape(1, SSD_W),
        pw_bd=pw_bd.astype(BF16), ps=pool_scale.reshape(1, POOL_W),
        wo=w_out.astype(BF16), g2=norm2.reshape(1, D_MODEL), wgu=w_gu.astype(BF16), wdn=w_down.astype(BF16))


def _head_mean_matrix():
    r = jnp.arange(ATT_W) // HEAD_DIM
    return jnp.where(r[:, None] == r[None, :], 1.0 / HEAD_DIM, 0.0).astype(BF16)


def _layer(x, p, e_mat, conv_prev, ssm_prev, pool_prev, attend, *, pos0, tm, is_prompt, kv_stack=None, layer=0):
    n, t, _ = x.shape
    x2d = x.reshape(n * t, D_MODEL)
    pr = _inproj(x2d, p["g1"], p["w_t"], p["qg"], p["kg"], e_mat, tm=tm, seq_len=t if is_prompt else None,
                 kv_stack=kv_stack, layer=layer)
    seq = lambda a: a.reshape(n, t, a.shape[-1])
    q, z, xbc, u, dt = [seq(pr[name]) for name in ("q", "z", "xbc", "u", "dt")]
    if is_prompt:
        kmean = pr["kmean"].reshape(n, t // MOBA_BLOCK, ATT_W)
        att = attend(q, seq(pr["k_aug"]), seq(pr["v_aug"]), kmean)
        k_out, v_out = pr["kt"], pr["vt"]
    else:
        att = attend(q, seq(pr["k"]), seq(pr["v"]), None)
        k_out, v_out = [pr[name].reshape(n, t, N_HEADS, HEAD_DIM) for name in ("k", "v")]

    conv_init = jnp.pad(conv_prev, ((0, 0), (_CONV_PAD - (CONV_W - 1), 0), (0, 0)))
    pool_init = jnp.pad(pool_prev, ((0, 0), (_POOL_PAD - POOL_HIST, 0), (0, 0)))
    y, pool, ssm_new = _mixer(xbc, dt, z, u, conv_init, pool_init, ssm_prev,
                              p["cw"], p["cb"], p["dtb"], p["alg"], p["dsk"], p["sg"], p["pw_bd"], p["ps"], pos0=pos0)
    y = y.reshape(n * t, SSD_W)
    pool = pool.reshape(n * t, POOL_W)

    out = _outffn(x2d, att.reshape(n * t, ATT_W), y, pool, p["wo"], p["g2"], p["wgu"], p["wdn"], tm=tm)
    conv_state = jnp.concatenate([conv_prev, xbc], axis=1)[:, -(CONV_W - 1):]
    pool_state = jnp.concatenate([pool_prev, u], axis=1)[:, -POOL_HIST:]
    return out.reshape(n, t, D_MODEL), k_out, v_out, conv_state, ssm_new, pool_state


def kernel(x_prompt, x_sample, cache_k, cache_v, page_table, state_ssm, state_conv, state_pool, norm1, w_in, q_norm, k_norm, conv_w, conv_b, dt_bias, a_log, d_skip, ssd_norm, pool_w, pool_scale, w_out, norm2, w_gu, w_down):
    depth = w_in.shape[0]
    b, s, _ = x_prompt.shape
    nd, ts, _ = x_sample.shape
    past_len = page_table.shape[1] * PAGE_SIZE
    e_mat = _head_mean_matrix()
    conv0 = jnp.zeros((b, CONV_W - 1, CONV_DIM), x_prompt.dtype)
    ssm0 = jnp.zeros((b, SSD_HEADS, SSD_HEADDIM, D_STATE), F32)
    pool0 = jnp.zeros((b, POOL_HIST, POOL_W), x_prompt.dtype)
    ck_t, cv_t = _cache_pages_t(cache_k), _cache_pages_t(cache_v)
    yp, ys = x_prompt, x_sample
    kv_stack = [jnp.zeros((depth, b, ATT_W, s), F32) for _ in range(2)]
    acc = [[] for _ in range(8)]
    for l in range(depth):
        p = _prep_layer_params(norm1[l], w_in[l], q_norm[l], k_norm[l], conv_w[l], conv_b[l], dt_bias[l],
                               a_log[l], d_skip[l], ssd_norm[l], pool_w[l], pool_scale[l], w_out[l],
                               norm2[l], w_gu[l], w_down[l])
        attend_p = lambda q, k, v, kmean: _prompt_attn(q, k, v, kmean)
        yp, *kv_stack, cp, hp, pp = _layer(yp, p, e_mat, conv0, ssm0, pool0, attend_p,
                                           pos0=0, tm=512, is_prompt=True, kv_stack=kv_stack, layer=l)
        attend_s = lambda q, k, v, kmean, l=l: _sample_attn(q, k, v, ck_t, cv_t, page_table, l)
        ys, kk, vv, cc, hh, pq = _layer(ys, p, e_mat, state_conv[l], state_ssm[l], state_pool[l], attend_s,
                                        pos0=past_len, tm=nd * ts, is_prompt=False)
        for lst, val in zip(acc, (kk, vv, hp, hh, cp, cc, pp, pq)):
            lst.append(val)
    kp, vp = [jnp.transpose(a.reshape(depth, b, N_HEADS, HEAD_DIM, s), (0, 1, 4, 2, 3)) for a in kv_stack]
    return (yp, ys, kp, vp) + tuple(jnp.stack(a) for a in acc)
```
